```python
import jax, jax.numpy as jnp
from jax import lax
import numpy as np

D_MODEL = 1024
BATCH = 16
SEQ = 2048
DEPTH = 1
DEC_BATCH = 128
DEC_SEQ = 8
PAST_LEN = 16384
PAGE_SIZE = 128

SB_HEADS = 8
SB_HEAD_DIM = 64
SB_W = SB_HEADS * SB_HEAD_DIM
MLA_HEADS = 8
MLA_NOPE = 64
MLA_ROPE = 32
MLA_V = 64
MLA_Q_RANK = 384
MLA_KV_RANK = 256
MLA_W = MLA_HEADS * MLA_V
MLA_SCALE = (MLA_NOPE + MLA_ROPE) ** -0.5
ROPE_BASE = 10000.0
MEM_TOKENS = 256
MEM_HEADS = 4
MEM_HEAD_DIM = 128
MEM_W = MEM_HEADS * MEM_HEAD_DIM
N_BRANCH = 3
D_FF = 2816
BLOCK_Q = 128
EPS = 1e-6

kernel_name = 'hybrid_stickbreak_mla_memory_macaron_step'


def rms_norm(x, g):
    xf = x.astype(jnp.float32)
    y = xf * lax.rsqrt(jnp.mean(xf * xf, axis=-1, keepdims=True) + EPS)
    return (y * g.astype(jnp.float32)).astype(x.dtype)


def swiglu(h, w_gate, w_up, w_down):
    return (jax.nn.silu(h @ w_gate) * (h @ w_up)) @ w_down


def rope_tables(pos):
    inv = ROPE_BASE ** (-jnp.arange(0, MLA_ROPE, 2, dtype=jnp.float32) / MLA_ROPE)
    ang = pos.astype(jnp.float32)[:, None] * inv[None, :]
    return jnp.cos(ang), jnp.sin(ang)


def apply_rope(x, cos, sin):
    half = MLA_ROPE // 2
    xf = x.astype(jnp.float32)
    x1, x2 = xf[..., :half], xf[..., half:]
    return jnp.concatenate([x1 * cos - x2 * sin, x2 * cos + x1 * sin], axis=-1).astype(x.dtype)


def project(h, pos, w_in, b_gate, q_norm, kv_norm, w_uq, w_uk):
    B, T, _ = h.shape
    sizes = (SB_W, SB_W, SB_W, MLA_Q_RANK, MLA_KV_RANK, MLA_ROPE, MEM_W, N_BRANCH * D_MODEL)
    offsets = [int(o) for o in np.cumsum(sizes)[:-1]]
    sb_q, sb_k, sb_v, cq, ckv, kpe, mq, g = jnp.split(h @ w_in, offsets, axis=-1)
    sb_q = sb_q.reshape(B, T, SB_HEADS, SB_HEAD_DIM)
    sb_k = sb_k.reshape(B, T, SB_HEADS, SB_HEAD_DIM)
    sb_v = sb_v.reshape(B, T, SB_HEADS, SB_HEAD_DIM)
    q = (rms_norm(cq, q_norm) @ w_uq).reshape(B, T, MLA_HEADS, MLA_NOPE + MLA_ROPE)
    q_nope, q_pe = q[..., :MLA_NOPE], q[..., MLA_NOPE:]
    q_lat = jnp.einsum('bthn,chn->bthc', q_nope, w_uk)
    ckv = rms_norm(ckv, kv_norm)
    cos, sin = rope_tables(pos)
    q_pe = apply_rope(q_pe, cos[:, None, :], sin[:, None, :])
    kpe = apply_rope(kpe, cos, sin)
    mq = mq.reshape(B, T, MEM_HEADS, MEM_HEAD_DIM)
    gate = jax.nn.sigmoid((g + b_gate).astype(jnp.float32)).astype(h.dtype)
    gate = gate.reshape(B, T, N_BRANCH, D_MODEL)
    return sb_q, sb_k, sb_v, q_lat, q_pe, ckv, kpe, mq, gate


def sb_update(q, k, v, valid, carry, acc):
    z = jnp.einsum('bqhd,bkhd->bhqk', q, k).astype(jnp.float32)
    log_stay = jnp.where(valid, jax.nn.log_sigmoid(-z), 0.0)
    later = lax.cumsum(log_stay, axis=3, reverse=True) - log_stay
    w = jnp.where(valid, jnp.exp(jax.nn.log_sigmoid(z) + later + carry[..., None]), 0.0)
    acc = acc + jnp.einsum('bhqk,bkhd->bhqd', w, v.astype(jnp.float32))
    carry = carry + log_stay.sum(axis=3)
    return carry, acc


def sb_prompt(q, k, v):
    B, S, H, d = q.shape
    nb = S // BLOCK_Q
    q = q * d ** -0.5

    def blocks(t):
        return t.reshape(B, nb, BLOCK_Q, H, d).swapaxes(0, 1)

    qb, kb, vb = blocks(q), blocks(k), blocks(v)
    idx = jnp.arange(nb)
    local = jnp.arange(BLOCK_Q)

    def query_block(args):
        qi, i = args
        qpos = i * BLOCK_Q + local

        def key_block(state, xs):
            kj, vj, j = xs
            valid = (j * BLOCK_Q + local)[None, :] < qpos[:, None]
            return sb_update(qi, kj, vj, valid, *state), None

        init = (jnp.zeros((B, H, BLOCK_Q), jnp.float32), jnp.zeros((B, H, BLOCK_Q, d), jnp.float32))
        (_, acc), _ = lax.scan(key_block, init, (kb, vb, idx), reverse=True)
        return acc

    o = lax.map(query_block, (qb, idx))
    return o.transpose(1, 0, 3, 2, 4).reshape(B, S, H * d).astype(q.dtype)


def sb_sample(q, k_new, v_new, cache_k, cache_v, page_table, layer):
    B, T, H, d = q.shape
    q = q * d ** -0.5
    local = jnp.arange(T)
    init = (jnp.zeros((B, H, T), jnp.float32), jnp.zeros((B, H, T, d), jnp.float32))
    state = sb_update(q, k_new, v_new, local[None, :] < local[:, None], *init)
    all_valid = jnp.ones((1, 1), dtype=bool)

    def page_step(state, pages):
        return sb_update(q, cache_k[layer, pages], cache_v[layer, pages], all_valid, *state), None

    (_, acc), _ = lax.scan(page_step, state, page_table.T, reverse=True)
    return acc.transpose(0, 2, 1, 3).reshape(B, T, H * d).astype(q.dtype)


def mla_scores(q_lat, q_pe, ckv, kpe):
    s = (jnp.einsum('bqhc,bkc->bhqk', q_lat, ckv).astype(jnp.float32)
         + jnp.einsum('bqhr,bkr->bhqk', q_pe, kpe).astype(jnp.float32))
    return s * MLA_SCALE


def mla_prompt(q_lat, q_pe, ckv, kpe):
    B, S = q_lat.shape[:2]
    nb = S // BLOCK_Q
    qlb = q_lat.reshape(B, nb, BLOCK_Q, MLA_HEADS, MLA_KV_RANK).swapaxes(0, 1)
    qpb = q_pe.reshape(B, nb, BLOCK_Q, MLA_HEADS, MLA_ROPE).swapaxes(0, 1)
    kpos = jnp.arange(S)
    local = jnp.arange(BLOCK_Q)
    ckv32 = ckv.astype(jnp.float32)

    def block(args):
        ql, qp, i = args
        s = mla_scores(ql, qp, ckv, kpe)
        valid = kpos[None, :] <= (i * BLOCK_Q + local)[:, None]
        p = jax.nn.softmax(jnp.where(valid, s, -jnp.inf), axis=-1)
        return jnp.einsum('bhqk,bkc->bqhc', p, ckv32)

    o = lax.map(block, (qlb, qpb, jnp.arange(nb)))
    return o.swapaxes(0, 1).reshape(B, S, MLA_HEADS, MLA_KV_RANK)


def mla_sample(q_lat, q_pe, ckv_new, kpe_new, cache_ckv, cache_kpe, page_table, layer):
    T = q_lat.shape[1]
    local = jnp.arange(T)
    s = jnp.where(local[None, :] <= local[:, None], mla_scores(q_lat, q_pe, ckv_new, kpe_new), -jnp.inf)
    m = s.max(axis=-1)
    p = jnp.exp(s - m[..., None])
    init = (m, p.sum(axis=-1), jnp.einsum('bhqk,bkc->bhqc', p, ckv_new.astype(jnp.float32)))

    def page_step(state, pages):
        m, l, acc = state
        ckv_p = cache_ckv[layer, pages].astype(jnp.float32)
        s = mla_scores(q_lat, q_pe, ckv_p, cache_kpe[layer, pages])
        m_new = jnp.maximum(m, s.max(axis=-1))
        corr = jnp.exp(m - m_new)
        p = jnp.exp(s - m_new[..., None])
        return (m_new, l * corr + p.sum(axis=-1),
                acc * corr[..., None] + jnp.einsum('bhqk,bkc->bhqc', p, ckv_p)), None

    (_, l, acc), _ = lax.scan(page_step, init, page_table.T)
    return (acc / l[..., None]).swapaxes(1, 2)


def mla_value(o_lat, w_uv, dtype):
    B, T = o_lat.shape[:2]
    o = jnp.einsum('bthc,chv->bthv', o_lat, w_uv.astype(jnp.float32))
    return o.reshape(B, T, MLA_W).astype(dtype)


def mem_kv(mem, g, w_mk, w_mv):
    mn = rms_norm(mem, g)
    return jnp.einsum('bmd,dhe->bmhe', mn, w_mk), jnp.einsum('bmd,dhe->bmhe', mn, w_mv)


def mem_attend(q, k, v):
    B, T = q.shape[:2]
    s = jnp.einsum('bthe,bmhe->bhtm', q, k).astype(jnp.float32) * MEM_HEAD_DIM ** -0.5
    p = jax.nn.softmax(s, axis=-1)
    o = jnp.einsum('bhtm,bmhe->bthe', p, v.astype(jnp.float32))
    return o.reshape(B, T, MEM_W).astype(q.dtype)


def merge(gate, o_sb, o_mla, o_mem, w_sb_out, w_mla_out, w_mem_out, w_o):
    m = (gate[..., 0, :] * (o_sb @ w_sb_out)
         + gate[..., 1, :] * (o_mla @ w_mla_out)
         + gate[..., 2, :] * (o_mem @ w_mem_out))
    return m @ w_o


def setup_inputs(seed: int = 0) -> dict:
    key = jax.random.key(seed)
    keys = iter(jax.random.split(key, 48))
    f32 = jnp.float32
    n_pages = PAST_LEN // PAGE_SIZE
    n_used = DEC_BATCH * n_pages
    n_pool = n_used + n_used // 4
    in_width = 3 * SB_W + MLA_Q_RANK + MLA_KV_RANK + MLA_ROPE + MEM_W + N_BRANCH * D_MODEL

    def rnd(shape):
        return jax.random.normal(next(keys), shape, f32)

    def w(shape, fan_in):
        return jax.random.normal(next(keys), shape, f32) * fan_in ** -0.5

    def gain(shape):
        return 1.0 + 0.05 * jax.random.normal(next(keys), shape, f32)

    page_table = jax.random.permutation(next(keys), n_pool)[:n_used].reshape(DEC_BATCH, n_pages).astype(jnp.int32)
    return {
        'x_prompt': rnd((BATCH, SEQ, D_MODEL)),
        'x_sample': rnd((DEC_BATCH, DEC_SEQ, D_MODEL)),
        'cache_sb_k': rnd((DEPTH, n_pool, PAGE_SIZE, SB_HEADS, SB_HEAD_DIM)),
        'cache_sb_v': rnd((DEPTH, n_pool, PAGE_SIZE, SB_HEADS, SB_HEAD_DIM)),
        'cache_mla_ckv': rnd((DEPTH, n_pool, PAGE_SIZE, MLA_KV_RANK)),
        'cache_mla_kpe': rnd((DEPTH, n_pool, PAGE_SIZE, MLA_ROPE)),
        'cache_mem_k': rnd((DEPTH, DEC_BATCH, MEM_TOKENS, MEM_HEADS, MEM_HEAD_DIM)),
        'cache_mem_v': rnd((DEPTH, DEC_BATCH, MEM_TOKENS, MEM_HEADS, MEM_HEAD_DIM)),
        'page_table': page_table,
        'mem_prompt': rnd((BATCH, MEM_TOKENS, D_MODEL)),
        'w_in': w((DEPTH, D_MODEL, in_width), D_MODEL),
        'b_gate': 0.01 * rnd((DEPTH, N_BRANCH * D_MODEL)),
        'q_norm': gain((DEPTH, MLA_Q_RANK)),
        'kv_norm': gain((DEPTH, MLA_KV_RANK)),
        'w_uq': w((DEPTH, MLA_Q_RANK, MLA_HEADS * (MLA_NOPE + MLA_ROPE)), MLA_Q_RANK),
        'w_uk': w((DEPTH, MLA_KV_RANK, MLA_HEADS, MLA_NOPE), MLA_KV_RANK),
        'w_uv': w((DEPTH, MLA_KV_RANK, MLA_HEADS, MLA_V), MLA_KV_RANK),
        'mem_norm': gain((DEPTH, D_MODEL)),
        'w_mk': w((DEPTH, D_MODEL, MEM_HEADS, MEM_HEAD_DIM), D_MODEL),
        'w_mv': w((DEPTH, D_MODEL, MEM_HEADS, MEM_HEAD_DIM), D_MODEL),
        'w_sb_out': w((DEPTH, SB_W, D_MODEL), SB_W),
        'w_mla_out': w((DEPTH, MLA_W, D_MODEL), MLA_W),
        'w_mem_out': w((DEPTH, MEM_W, D_MODEL), MEM_W),
        'w_o': w((DEPTH, D_MODEL, D_MODEL), D_MODEL),
        'norm_ffn1': gain((DEPTH, D_MODEL)),
        'w1_gate': w((DEPTH, D_MODEL, D_FF), D_MODEL),
        'w1_up': w((DEPTH, D_MODEL, D_FF), D_MODEL),
        'w1_down': w((DEPTH, D_FF, D_MODEL), D_FF),
        'norm_mix': gain((DEPTH, D_MODEL)),
        'norm_ffn2': gain((DEPTH, D_MODEL)),
        'w2_gate': w((DEPTH, D_MODEL, D_FF), D_MODEL),
        'w2_up': w((DEPTH, D_MODEL, D_FF), D_MODEL),
        'w2_down': w((DEPTH, D_FF, D_MODEL), D_FF),
        'norm_final': gain((D_MODEL,)),
    }


def reference(x_prompt, x_sample, cache_sb_k, cache_sb_v, cache_mla_ckv, cache_mla_kpe,
              cache_mem_k, cache_mem_v, page_table, mem_prompt,
              w_in, b_gate, q_norm, kv_norm, w_uq, w_uk, w_uv, mem_norm, w_mk, w_mv,
              w_sb_out, w_mla_out, w_mem_out, w_o,
              norm_ffn1, w1_gate, w1_up, w1_down, norm_mix, norm_ffn2, w2_gate, w2_up, w2_down,
              norm_final):
    pos_p = jnp.arange(x_prompt.shape[1])
    pos_s = PAST_LEN + jnp.arange(x_sample.shape[1])
    xp, xs = x_prompt, x_sample
    sbk_p, sbv_p, ckv_p, kpe_p, mk_p, mv_p = [], [], [], [], [], []
    sbk_s, sbv_s, ckv_s, kpe_s = [], [], [], []
    for l in range(DEPTH):
        xp = xp + 0.5 * swiglu(rms_norm(xp, norm_ffn1[l]), w1_gate[l], w1_up[l], w1_down[l])
        xs = xs + 0.5 * swiglu(rms_norm(xs, norm_ffn1[l]), w1_gate[l], w1_up[l], w1_down[l])

        sq, sk, sv, ql, qp, ckv, kpe, mq, gate = project(
            rms_norm(xp, norm_mix[l]), pos_p, w_in[l], b_gate[l], q_norm[l], kv_norm[l], w_uq[l], w_uk[l])
        o_sb = sb_prompt(sq, sk, sv)
        o_mla = mla_value(mla_prompt(ql, qp, ckv, kpe), w_uv[l], xp.dtype)
        mk, mv = mem_kv(mem_prompt, mem_norm[l], w_mk[l], w_mv[l])
        o_mem = mem_attend(mq, mk, mv)
        xp = xp + merge(gate, o_sb, o_mla, o_mem, w_sb_out[l], w_mla_out[l], w_mem_out[l], w_o[l])
        sbk_p.append(sk); sbv_p.append(sv); ckv_p.append(ckv); kpe_p.append(kpe)
        mk_p.append(mk); mv_p.append(mv)

        sq, sk, sv, ql, qp, ckv, kpe, mq, gate = project(
            rms_norm(xs, norm_mix[l]), pos_s, w_in[l], b_gate[l], q_norm[l], kv_norm[l], w_uq[l], w_uk[l])
        o_sb = sb_sample(sq, sk, sv, cache_sb_k, cache_sb_v, page_table, l)
        o_lat = mla_sample(ql, qp, ckv, kpe, cache_mla_ckv, cache_mla_kpe, page_table, l)
        o_mla = mla_value(o_lat, w_uv[l], xs.dtype)
        o_mem = mem_attend(mq, cache_mem_k[l], cache_mem_v[l])
        xs = xs + merge(gate, o_sb, o_mla, o_mem, w_sb_out[l], w_mla_out[l], w_mem_out[l], w_o[l])
        sbk_s.append(sk); sbv_s.append(sv); ckv_s.append(ckv); kpe_s.append(kpe)

        xp = xp + 0.5 * swiglu(rms_norm(xp, norm_ffn2[l]), w2_gate[l], w2_up[l], w2_down[l])
        xs = xs + 0.5 * swiglu(rms_norm(xs, norm_ffn2[l]), w2_gate[l], w2_up[l], w2_down[l])

    y_prompt = rms_norm(xp, norm_final)
    y_sample = rms_norm(xs, norm_final)
    sb_k_prompt = jnp.stack(sbk_p)
    sb_v_prompt = jnp.stack(sbv_p)
    mla_ckv_prompt = jnp.stack(ckv_p)
    mla_kpe_prompt = jnp.stack(kpe_p)
    mem_k_prompt = jnp.stack(mk_p)
    mem_v_prompt = jnp.stack(mv_p)
    sb_k_sample = jnp.stack(sbk_s)
    sb_v_sample = jnp.stack(sbv_s)
    mla_ckv_sample = jnp.stack(ckv_s)
    mla_kpe_sample = jnp.stack(kpe_s)
    return (y_prompt, y_sample, sb_k_prompt, sb_v_prompt, mla_ckv_prompt, mla_kpe_prompt,
            mem_k_prompt, mem_v_prompt, sb_k_sample, sb_v_sample, mla_ckv_sample, mla_kpe_sample)
```

```python
import functools

import jax
import jax.numpy as jnp
from jax import lax
from jax.experimental import pallas as pl
from jax.experimental.pallas import tpu as pltpu

EPS = 1e-6
ROPE_BASE = 10000.0
LANES = 128
NEG_BIG = -1e30
F32 = jnp.float32
BF16 = jnp.bfloat16
VMEM_LIMIT = 56 * 1024 * 1024

NT_DIMS = (((1,), (1,)), ((), ()))


def _dot(a, b):
    return jnp.dot(a, b, preferred_element_type=F32)


def _dot_nt(a, b):
    return lax.dot_general(a, b, NT_DIMS, preferred_element_type=F32)


def _rms(x, g):
    return x * lax.rsqrt(jnp.mean(x * x, axis=-1, keepdims=True) + EPS) * g


def _const_spec(shape):
    zeros = (0,) * len(shape)
    return pl.BlockSpec(shape, lambda *_: zeros, pipeline_mode=pl.Buffered(1))


def _params(*sem):
    return pltpu.CompilerParams(dimension_semantics=sem, vmem_limit_bytes=VMEM_LIMIT)


def _largest_divisor(n, candidates):
    for c in candidates:
        if n % c == 0:
            return c
    return n


def _lane_tile(x, reps):
    return x if reps == 1 else jnp.concatenate([x] * reps, axis=1)


def _ffn_kernel(x_ref, g_ref, wg_ref, wu_ref, wd_ref, gf_ref, o_ref, *, ff_chunk, final):
    x = x_ref[...]
    h = _rms(x, g_ref[...]).astype(BF16)
    acc = jnp.zeros_like(x)
    for c in range(0, wg_ref.shape[1], ff_chunk):
        a = _dot(h, wg_ref[:, c:c + ff_chunk])
        u = _dot(h, wu_ref[:, c:c + ff_chunk])
        s = (a * jax.nn.sigmoid(a) * u).astype(BF16)
        acc = acc + _dot(s, wd_ref[c:c + ff_chunk, :])
    y = x + 0.5 * acc
    if final:
        y = _rms(y, gf_ref[...])
    o_ref[...] = y


def _ffn(x, g, wg, wu, wd, gf, *, final):
    n, d = x.shape
    f = wg.shape[1]
    tm = _largest_divisor(n, (512, 256, 128, 64, 32, 16, 8))
    kern = functools.partial(_ffn_kernel, ff_chunk=_largest_divisor(f, (512, 256, 128)), final=final)
    return pl.pallas_call(
        kern,
        grid=(n // tm,),
        in_specs=[pl.BlockSpec((tm, d), lambda i: (i, 0)), _const_spec((1, d)), _const_spec((d, f)),
                  _const_spec((d, f)), _const_spec((f, d)), _const_spec((1, d))],
        out_specs=pl.BlockSpec((tm, d), lambda i: (i, 0)),
        out_shape=jax.ShapeDtypeStruct((n, d), F32),
        compiler_params=_params("parallel"),
        name="ffn",
    )(x, g, wg, wu, wd, gf)


def _proj_kernel(x_ref, gmix_ref, wt_ref, wrest_ref, wg_ref, bg_ref, qn_ref, kvn_ref, wuqn_ref, wuqp_ref,
                 wuqs_ref, wuk_ref, cosk_ref, sink_ref, cost_ref, sint_ref,
                 qt_ref, kt_ref, vt_ref, kpet_ref, ckv_ref, kpe_ref, qlat_ref, qpe_ref, mq_ref, gate_ref,
                 *, sbw, rope, q_rank, kv_rank, mem_w, sb_scale, mla_scale, mem_scale, gate_chunk):
    h = _rms(x_ref[...], gmix_ref[...]).astype(BF16)

    t = _dot_nt(wt_ref[...], h)
    qt_ref[0] = t[0:sbw] * sb_scale
    kt_ref[0] = t[sbw:2 * sbw]
    vt_ref[0] = t[2 * sbw:3 * sbw]
    o = 3 * sbw
    kpet_ref[0] = t[o:o + rope] * cost_ref[...] + t[o + rope:o + 2 * rope] * sint_ref[...]

    r = _dot(h, wrest_ref[...])
    cq = r[:, 0:q_rank]
    o = q_rank
    ckv_ref[...] = _rms(r[:, o:o + kv_rank], kvn_ref[...])
    o += kv_rank
    cosk = cosk_ref[...]
    sink = sink_ref[...]
    kpe_ref[...] = (r[:, o:o + LANES] * cosk + r[:, o + LANES:o + 2 * LANES] * sink).astype(kpe_ref.dtype)
    o += 2 * LANES
    mq_ref[...] = r[:, o:o + mem_w] * mem_scale

    cqn = _rms(cq, qn_ref[...]).astype(BF16)
    heads = qlat_ref.shape[0]
    qpe = (_dot(cqn, wuqp_ref[...]) * _lane_tile(cosk, heads)
           + _dot(cqn, wuqs_ref[...]) * _lane_tile(sink, heads)) * mla_scale
    qn = _dot(cqn, wuqn_ref[...])
    nope = LANES // 2
    first = lax.broadcasted_iota(jnp.int32, (qn.shape[0], LANES), 1) < nope
    for hd in range(heads):
        qpe_ref[hd] = qpe[:, hd * LANES:(hd + 1) * LANES].astype(qpe_ref.dtype)
        pair = qn[:, (hd // 2) * LANES:(hd // 2 + 1) * LANES]
        keep = first if hd % 2 == 0 else jnp.logical_not(first)
        lhs = jnp.where(keep, pair, 0.0).astype(BF16)
        qlat_ref[hd] = (_dot(lhs, wuk_ref[hd // 2]) * mla_scale).astype(qlat_ref.dtype)

    for c in range(0, wg_ref.shape[1], gate_chunk):
        g = _dot(h, wg_ref[:, c:c + gate_chunk]) + bg_ref[:, c:c + gate_chunk]
        gate_ref[:, c:c + gate_chunk] = jax.nn.sigmoid(g).astype(gate_ref.dtype)


def _project(x, nb, pos, w, *, act_dtype, dims):
    n, d = x.shape
    s = n // nb
    sbw, rope, q_rank, kv_rank, mem_w, heads = (dims[k] for k in ("sbw", "rope", "q_rank", "kv_rank", "mem_w", "mla_heads"))
    assert sbw % 8 == 0 and rope % 8 == 0 and 2 * dims["mla_nope"] == LANES and heads % 2 == 0
    tm = _largest_divisor(s, (256, 128, 64, 32, 16, 8))
    ns = s // tm

    inv = ROPE_BASE ** (-jnp.arange(0, rope, 2, dtype=F32) / rope)
    ang = pos.astype(F32)[:, None] * inv[None, :]
    cos, sin = jnp.cos(ang), jnp.sin(ang)
    pad = jnp.zeros((s, LANES - rope), F32)
    cosk = jnp.concatenate([cos, cos, pad], axis=1)
    sink = jnp.concatenate([-sin, sin, pad], axis=1)
    cost = jnp.concatenate([cos, cos], axis=1).T
    sint = jnp.concatenate([-sin, sin], axis=1).T

    g3 = w["w_g"].shape[1]
    kern = functools.partial(
        _proj_kernel, sbw=sbw, rope=rope, q_rank=q_rank, kv_rank=kv_rank, mem_w=mem_w,
        sb_scale=dims["sb_d"] ** -0.5, mla_scale=(dims["mla_nope"] + rope) ** -0.5, mem_scale=dims["mem_d"] ** -0.5,
        gate_chunk=_largest_divisor(g3, (512, 256, 128)))
    consts = [w["norm_mix"], w["w_t"], w["w_rest"], w["w_g"], w["b_g"], w["q_norm"], w["kv_norm"],
              w["w_uq_nope"], w["w_uq_pe"], w["w_uq_pe_sw"], w["w_uk_pairs"]]
    tok = lambda b, i: (b * ns + i, 0)
    tok3 = lambda b, i: (0, b * ns + i, 0)
    tr = lambda b, i: (b, 0, i)
    in_specs = ([pl.BlockSpec((tm, d), tok)] + [_const_spec(c.shape) for c in consts]
                + [pl.BlockSpec((tm, LANES), lambda b, i: (i, 0))] * 2
                + [pl.BlockSpec((rope, tm), lambda b, i: (0, i))] * 2)
    out_shape = [jax.ShapeDtypeStruct((nb, sbw, s), F32)] * 3 + [
        jax.ShapeDtypeStruct((nb, rope, s), F32),
        jax.ShapeDtypeStruct((n, kv_rank), F32),
        jax.ShapeDtypeStruct((n, LANES), act_dtype),
        jax.ShapeDtypeStruct((heads, n, kv_rank), act_dtype),
        jax.ShapeDtypeStruct((heads, n, LANES), act_dtype),
        jax.ShapeDtypeStruct((n, mem_w), F32),
        jax.ShapeDtypeStruct((n, g3), BF16)]
    out_specs = [pl.BlockSpec((1, sbw, tm), tr)] * 3 + [
        pl.BlockSpec((1, rope, tm), tr),
        pl.BlockSpec((tm, kv_rank), tok),
        pl.BlockSpec((tm, LANES), tok),
        pl.BlockSpec((heads, tm, kv_rank), tok3),
        pl.BlockSpec((heads, tm, LANES), tok3),
        pl.BlockSpec((tm, mem_w), tok),
        pl.BlockSpec((tm, g3), tok)]
    return pl.pallas_call(
        kern, grid=(nb, ns), in_specs=in_specs, out_specs=out_specs, out_shape=out_shape,
        compiler_params=_params("parallel", "parallel"), name="project",
    )(x, *consts, cosk, sink, cost, sint)


def _sb_weights(z, carry, tcat, valid):
    tk = z.shape[1]
    nz = -z
    log_stay_all = jnp.minimum(nz, 0.0) - jnp.log(1.0 + jnp.exp(jnp.minimum(z, nz)))
    log_stay = log_stay_all if valid is None else jnp.where(valid, log_stay_all, 0.0)
    hi = log_stay.astype(BF16)
    lo = (log_stay - hi.astype(F32)).astype(BF16)
    sums = _dot(jnp.concatenate([hi, lo], axis=1), tcat)
    later = sums[:, :tk]
    total = sums[:, tk:]
    w = jnp.exp(z + log_stay_all + later + _lane_tile(carry, tk // LANES))
    if valid is not None:
        w = jnp.where(valid, w, 0.0)
    return w, carry + total


def _sb_tcat(tk):
    s = lax.broadcasted_iota(jnp.int32, (tk, tk), 0)
    k = lax.broadcasted_iota(jnp.int32, (tk, tk), 1)
    half = jnp.concatenate([(s > k).astype(BF16), jnp.ones((tk, LANES), BF16)], axis=1)
    return jnp.concatenate([half, half], axis=0)


def _sb_prompt_kernel(qt_ref, kt_ref, vt_ref, tcat_ref, o_ref, q_sc, carry_sc, acc_sc, *, pairs):
    i = pl.program_id(1)
    j = pl.program_id(2)
    tq = qt_ref.shape[2]
    tk = kt_ref.shape[2]

    @pl.when(j == 0)
    def _():
        first = lax.broadcasted_iota(jnp.int32, (tq, LANES), 1) < LANES // 2
        for p in range(pairs):
            qp = qt_ref[0, p * LANES:(p + 1) * LANES, :].T
            q_sc[2 * p] = jnp.where(first, qp, 0.0).astype(BF16)
            q_sc[2 * p + 1] = jnp.where(first, 0.0, qp).astype(BF16)
        carry_sc[...] = jnp.zeros_like(carry_sc)
        acc_sc[...] = jnp.zeros_like(acc_sc)

    def step(masked):
        valid = None
        if masked:
            valid = (lax.broadcasted_iota(jnp.int32, (tq, tk), 1)
                     < lax.broadcasted_iota(jnp.int32, (tq, tk), 0))
        first = lax.broadcasted_iota(jnp.int32, (tq, LANES), 1) < LANES // 2
        tcat = tcat_ref[...]
        for p in range(pairs):
            kp = kt_ref[0, p * LANES:(p + 1) * LANES, :].astype(BF16)
            vp = vt_ref[0, p * LANES:(p + 1) * LANES, :].astype(BF16)
            pv = []
            for hh in range(2):
                hd = 2 * p + hh
                z = _dot(q_sc[hd], kp)
                w, carry = _sb_weights(z, carry_sc[hd], tcat, valid)
                carry_sc[hd] = carry
                pv.append(_dot_nt(w.astype(BF16), vp))
            acc_sc[p] += jnp.where(first, pv[0], pv[1])

    pl.when(j == 0)(lambda: step(True))
    pl.when(jnp.logical_and(j > 0, j <= i))(lambda: step(False))

    @pl.when(j == pl.num_programs(2) - 1)
    def _():
        for p in range(pairs):
            o_ref[:, p * LANES:(p + 1) * LANES] = acc_sc[p].astype(o_ref.dtype)


def _sb_prompt(qt, kt, vt):
    nb, sbw, s = qt.shape
    assert sbw % LANES == 0
    pairs = sbw // LANES
    t = _largest_divisor(s, (256, 128))
    nq = s // t
    kern = functools.partial(_sb_prompt_kernel, pairs=pairs)
    kv_map = lambda b, i, j: (b, 0, jnp.maximum(i - j, 0))
    return pl.pallas_call(
        kern, grid=(nb, nq, nq),
        in_specs=[pl.BlockSpec((1, sbw, t), lambda b, i, j: (b, 0, i)),
                  pl.BlockSpec((1, sbw, t), kv_map), pl.BlockSpec((1, sbw, t), kv_map),
                  _const_spec((2 * t, t + LANES))],
        out_specs=pl.BlockSpec((t, sbw), lambda b, i, j: (b * nq + i, 0)),
        out_shape=jax.ShapeDtypeStruct((nb * s, sbw), BF16),
        scratch_shapes=[pltpu.VMEM((2 * pairs, t, LANES), BF16), pltpu.VMEM((2 * pairs, t, LANES), F32),
                        pltpu.VMEM((pairs, t, LANES), F32)],
        compiler_params=_params("parallel", "parallel", "arbitrary"), name="sb_prompt",
    )(qt, kt, vt, _sb_tcat(t))


def _softmax_update(s, v, m_sc, l_sc, acc_sc):
    reps_k = s.shape[1] // LANES
    m_prev = m_sc[...]
    m_new = jnp.maximum(m_prev, jnp.max(s, axis=1, keepdims=True))
    p = jnp.exp(s - _lane_tile(m_new, reps_k))
    alpha = jnp.exp(m_prev - m_new)
    l_sc[...] = alpha * l_sc[...] + jnp.sum(p, axis=1, keepdims=True)
    acc_sc[...] = acc_sc[...] * _lane_tile(alpha, acc_sc.shape[1] // LANES) + _dot(p.astype(BF16), v)
    m_sc[...] = m_new


def _mla_prompt_kernel(qlat_ref, qpe_ref, ckv_ref, kpe_ref, o_ref, m_sc, l_sc, acc_sc):
    i = pl.program_id(1)
    j = pl.program_id(2)
    heads, tq, c = qlat_ref.shape
    tk = ckv_ref.shape[0]

    @pl.when(j == 0)
    def _():
        m_sc[...] = jnp.full_like(m_sc, NEG_BIG)
        l_sc[...] = jnp.zeros_like(l_sc)
        acc_sc[...] = jnp.zeros_like(acc_sc)

    def step(masked):
        ckv = ckv_ref[...].astype(BF16)
        s = (_dot_nt(qlat_ref[...].reshape(heads * tq, c), ckv)
             + _dot_nt(qpe_ref[...].reshape(heads * tq, LANES), kpe_ref[...]))
        if masked:
            s = s.reshape(heads, tq, tk)
            key = lax.broadcasted_iota(jnp.int32, (heads, tq, tk), 2)
            qry = lax.broadcasted_iota(jnp.int32, (heads, tq, tk), 1)
            s = jnp.where(key <= qry, s, -jnp.inf).reshape(heads * tq, tk)
        _softmax_update(s, ckv, m_sc, l_sc, acc_sc)

    pl.when(j < i)(lambda: step(False))
    pl.when(j == i)(lambda: step(True))

    @pl.when(j == pl.num_programs(2) - 1)
    def _():
        o = acc_sc[...] / _lane_tile(l_sc[...], c // LANES)
        o_ref[...] = o.reshape(heads, tq, c).astype(o_ref.dtype)


def _mla_prompt(qlat, qpe, ckv, kpe, nb):
    heads, n, c = qlat.shape
    assert c % LANES == 0
    s = n // nb
    t = _largest_divisor(s, (256, 128))
    nq = s // t
    q_map = lambda b, i, j: (0, b * nq + i, 0)
    k_map = lambda b, i, j: (b * nq + jnp.minimum(j, i), 0)
    return pl.pallas_call(
        _mla_prompt_kernel, grid=(nb, nq, nq),
        in_specs=[pl.BlockSpec((heads, t, c), q_map), pl.BlockSpec((heads, t, LANES), q_map),
                  pl.BlockSpec((t, c), k_map), pl.BlockSpec((t, LANES), k_map)],
        out_specs=pl.BlockSpec((heads, t, c), q_map),
        out_shape=jax.ShapeDtypeStruct((heads, n, c), BF16),
        scratch_shapes=[pltpu.VMEM((heads * t, LANES), F32), pltpu.VMEM((heads * t, LANES), F32),
                        pltpu.VMEM((heads * t, c), F32)],
        compiler_params=_params("parallel", "parallel", "arbitrary"), name="mla_prompt",
    )(qlat, qpe, ckv, kpe)


def _mem_kv_kernel(mem_ref, g_ref, wk_ref, wv_ref, k_ref, v_ref):
    mn = _rms(mem_ref[...], g_ref[...]).astype(BF16)
    k_ref[...] = _dot(mn, wk_ref[...])
    v_ref[...] = _dot(mn, wv_ref[...])


def _mem_kv(mem, g, wk, wv):
    n, d = mem.shape
    mw = wk.shape[1]
    tm = _largest_divisor(n, (512, 256, 128, 64, 32, 16, 8))
    row = lambda i: (i, 0)
    return pl.pallas_call(
        _mem_kv_kernel, grid=(n // tm,),
        in_specs=[pl.BlockSpec((tm, d), row), _const_spec((1, d)), _const_spec((d, mw)), _const_spec((d, mw))],
        out_specs=[pl.BlockSpec((tm, mw), row)] * 2,
        out_shape=[jax.ShapeDtypeStruct((n, mw), F32)] * 2,
        compiler_params=_params("parallel"), name="mem_kv",
    )(mem, g, wk, wv)


def _mem_attend_kernel(q_ref, k_ref, v_ref, o_ref, *, head_dim):
    for c in range(0, q_ref.shape[1], head_dim):
        q = q_ref[:, c:c + head_dim].astype(BF16)
        s = _dot_nt(q, k_ref[0, :, c:c + head_dim].astype(BF16))
        p = jnp.exp(s - jnp.max(s, axis=1, keepdims=True))
        o = _dot(p.astype(BF16), v_ref[0, :, c:c + head_dim].astype(BF16))
        o_ref[:, c:c + head_dim] = o / jnp.sum(p, axis=1, keepdims=True)


def _mem_attend(mq, mk, mv, *, head_dim):
    n, mw = mq.shape
    nb, m, _ = mk.shape
    assert head_dim % LANES == 0
    t = n // nb
    tq = _largest_divisor(t, (512, 256, 128, 64, 32, 16, 8))
    nt = t // tq
    kern = functools.partial(_mem_attend_kernel, head_dim=head_dim)
    return pl.pallas_call(
        kern, grid=(nb, nt),
        in_specs=[pl.BlockSpec((tq, mw), lambda b, i: (b * nt + i, 0)),
                  pl.BlockSpec((1, m, mw), lambda b, i: (b, 0, 0)), pl.BlockSpec((1, m, mw), lambda b, i: (b, 0, 0))],
        out_specs=pl.BlockSpec((tq, mw), lambda b, i: (b * nt + i, 0)),
        out_shape=jax.ShapeDtypeStruct((n, mw), F32),
        compiler_params=_params("parallel", "parallel"), name="mem_attend",
    )(mq, mk, mv)


def _merge_kernel(x_ref, gate_ref, osb_ref, olat_ref, omem_ref, wuv_ref, wsb_ref, wmla_ref, wmem_ref, wo_ref, o_ref):
    d = x_ref.shape[1]
    heads = olat_ref.shape[0]
    parts = []
    for p in range(heads // 2):
        lhs = jnp.concatenate([olat_ref[2 * p].astype(BF16), olat_ref[2 * p + 1].astype(BF16)], axis=1)
        parts.append(_dot(lhs, wuv_ref[p]))
    o_mla = jnp.concatenate(parts, axis=1).astype(BF16)
    m = (gate_ref[:, 0:d].astype(F32) * _dot(osb_ref[...].astype(BF16), wsb_ref[...])
         + gate_ref[:, d:2 * d].astype(F32) * _dot(o_mla, wmla_ref[...])
         + gate_ref[:, 2 * d:3 * d].astype(F32) * _dot(omem_ref[...].astype(BF16), wmem_ref[...]))
    o_ref[...] = x_ref[...] + _dot(m.astype(BF16), wo_ref[...])


def _merge(x, gate, o_sb, o_lat, o_mem, w):
    n, d = x.shape
    heads, _, c = o_lat.shape
    tm = _largest_divisor(n, (512, 256, 128, 64, 32, 16, 8))
    row = lambda i: (i, 0)
    consts = [w["w_uv_pairs"], w["w_sb_out"], w["w_mla_out"], w["w_mem_out"], w["w_o"]]
    return pl.pallas_call(
        _merge_kernel, grid=(n // tm,),
        in_specs=[pl.BlockSpec((tm, d), row), pl.BlockSpec((tm, gate.shape[1]), row),
                  pl.BlockSpec((tm, o_sb.shape[1]), row), pl.BlockSpec((heads, tm, c), lambda i: (0, i, 0)),
                  pl.BlockSpec((tm, o_mem.shape[1]), row)] + [_const_spec(a.shape) for a in consts],
        out_specs=pl.BlockSpec((tm, d), row),
        out_shape=jax.ShapeDtypeStruct((n, d), F32),
        compiler_params=_params("parallel"), name="merge",
    )(x, gate, o_sb, o_lat, o_mem, *consts)


def _sb_sample_kernel(pt_ref, q_ref, knew_ref, vnew_ref, tcat_ref, *rest, group, heads):
    k_refs = rest[:group]
    v_refs = rest[group:2 * group]
    o_ref, qbd_sc, carry_sc, acc_sc = rest[2 * group:]
    s = pl.program_id(1)
    t, sbw = q_ref.shape
    rows = heads * t
    hd = sbw // heads
    tk = knew_ref.shape[2]
    tcat = tcat_ref[...]

    def own_head():
        row = lax.broadcasted_iota(jnp.int32, (rows, sbw), 0)
        col = lax.broadcasted_iota(jnp.int32, (rows, sbw), 1)
        mask = None
        for h in range(heads):
            m = (row >= h * t) & (row < (h + 1) * t) & (col >= h * hd) & (col < (h + 1) * hd)
            mask = m if mask is None else mask | m
        return mask

    def block(k, v, valid):
        z = _dot(qbd_sc[...], k.astype(BF16))
        w, carry = _sb_weights(z, carry_sc[...], tcat, valid)
        carry_sc[...] = carry
        acc_sc[...] += _dot_nt(w.astype(BF16), v.astype(BF16))

    @pl.when(s == 0)
    def _():
        q = jnp.concatenate([q_ref[...]] * heads, axis=0)
        qbd_sc[...] = jnp.where(own_head(), q, 0.0).astype(BF16)
        carry_sc[...] = jnp.zeros_like(carry_sc)
        acc_sc[...] = jnp.zeros_like(acc_sc)
        key = lax.broadcasted_iota(jnp.int32, (rows, tk), 1)
        qry = lax.rem(lax.broadcasted_iota(jnp.int32, (rows, tk), 0), t)
        block(knew_ref[0], vnew_ref[0], key < qry)

    for g in range(group):
        block(k_refs[g][...], v_refs[g][...], None)

    @pl.when(s == pl.num_programs(1) - 1)
    def _():
        diag = jnp.where(own_head(), acc_sc[...], 0.0).reshape(heads, t, sbw)
        o_ref[...] = jnp.sum(diag, axis=0)


def _sb_sample(q, knew, vnew, cache_k, cache_v, page_table, *, heads):
    n, sbw = q.shape
    nb, n_pages = page_table.shape
    t = n // nb
    page = cache_k.shape[2]
    assert page % LANES == 0
    group = _largest_divisor(n_pages, (8, 4, 2, 1))
    steps = n_pages // group

    def page_spec(g):
        def index(b, s, pt):
            return (pt[b * n_pages + n_pages - 1 - (s * group + g)], 0, 0)
        return pl.BlockSpec((None, sbw, page), index)

    own = lambda b, s, pt: (b, 0)
    new = lambda b, s, pt: (b, 0, 0)
    kern = functools.partial(_sb_sample_kernel, group=group, heads=heads)
    grid_spec = pltpu.PrefetchScalarGridSpec(
        num_scalar_prefetch=1, grid=(nb, steps),
        in_specs=[pl.BlockSpec((t, sbw), own), pl.BlockSpec((1, sbw, page), new), pl.BlockSpec((1, sbw, page), new),
                  pl.BlockSpec((2 * page, page + LANES), lambda b, s, pt: (0, 0))]
        + [page_spec(g) for g in range(group)] * 2,
        out_specs=pl.BlockSpec((t, sbw), own),
        scratch_shapes=[pltpu.VMEM((heads * t, sbw), BF16), pltpu.VMEM((heads * t, LANES), F32),
                        pltpu.VMEM((heads * t, sbw), F32)])
    return pl.pallas_call(
        kern, grid_spec=grid_spec, out_shape=jax.ShapeDtypeStruct((n, sbw), F32),
        compiler_params=_params("parallel", "arbitrary"), name="sb_sample",
    )(page_table.reshape(-1), q, knew, vnew, _sb_tcat(page), *([cache_k] * group), *([cache_v] * group))


def _mla_sample_kernel(pt_ref, qlat_ref, qpe_ref, cnew_ref, pnew_ref, *rest, group):
    c_refs = rest[:group]
    p_refs = rest[group:2 * group]
    o_ref, m_sc, l_sc, acc_sc = rest[2 * group:]
    s = pl.program_id(1)
    heads, t, c = qlat_ref.shape
    rows = heads * t
    rope, tk = pnew_ref.shape[1:]
    qlat = qlat_ref[...].reshape(rows, c).astype(BF16)
    qpe = qpe_ref[...].reshape(rows, LANES).astype(BF16)
    zpad = jnp.zeros((LANES - rope, tk), BF16)

    def scores(ckv, kpet):
        kpe = jnp.concatenate([kpet.astype(BF16), zpad], axis=0)
        return _dot_nt(qlat, ckv) + _dot(qpe, kpe)

    @pl.when(s == 0)
    def _():
        m_sc[...] = jnp.full_like(m_sc, NEG_BIG)
        l_sc[...] = jnp.zeros_like(l_sc)
        acc_sc[...] = jnp.zeros_like(acc_sc)
        ckv = cnew_ref[0].astype(BF16)
        key = lax.broadcasted_iota(jnp.int32, (rows, tk), 1)
        qry = lax.rem(lax.broadcasted_iota(jnp.int32, (rows, tk), 0), t)
        sc = jnp.where(key <= qry, scores(ckv, pnew_ref[0]), -jnp.inf)
        _softmax_update(sc, ckv, m_sc, l_sc, acc_sc)

    ckvs = [r[...].astype(BF16) for r in c_refs]
    sc = jnp.concatenate([scores(ckvs[g], p_refs[g][...]) for g in range(group)], axis=1)
    _softmax_update(sc, jnp.concatenate(ckvs, axis=0), m_sc, l_sc, acc_sc)

    @pl.when(s == pl.num_programs(1) - 1)
    def _():
        o = acc_sc[...] / _lane_tile(l_sc[...], c // LANES)
        o_ref[...] = o.reshape(heads, t, c)


def _mla_sample(qlat, qpe, cnew, pnew, cache_ckv, cache_kpet, page_table):
    heads, n, c = qlat.shape
    nb, n_pages = page_table.shape
    t = n // nb
    page = cache_ckv.shape[1]
    rope = cache_kpet.shape[1]
    group = _largest_divisor(n_pages, (8, 4, 2, 1))
    steps = n_pages // group

    def page_spec(shape, g):
        def index(b, s, pt):
            return (pt[b * n_pages + s * group + g], 0, 0)
        return pl.BlockSpec((None,) + shape, index)

    own = lambda b, s, pt: (0, b, 0)
    new = lambda b, s, pt: (b, 0, 0)
    kern = functools.partial(_mla_sample_kernel, group=group)
    grid_spec = pltpu.PrefetchScalarGridSpec(
        num_scalar_prefetch=1, grid=(nb, steps),
        in_specs=[pl.BlockSpec((heads, t, c), own), pl.BlockSpec((heads, t, LANES), own),
                  pl.BlockSpec((1, page, c), new), pl.BlockSpec((1, rope, page), new)]
        + [page_spec((page, c), g) for g in range(group)] + [page_spec((rope, page), g) for g in range(group)],
        out_specs=pl.BlockSpec((heads, t, c), own),
        scratch_shapes=[pltpu.VMEM((heads * t, LANES), F32), pltpu.VMEM((heads * t, LANES), F32),
                        pltpu.VMEM((heads * t, c), F32)])
    return pl.pallas_call(
        kern, grid_spec=grid_spec, out_shape=jax.ShapeDtypeStruct((heads, n, c), F32),
        compiler_params=_params("parallel", "arbitrary"), name="mla_sample",
    )(page_table.reshape(-1), qlat, qpe, cnew, pnew, *([cache_ckv] * group), *([cache_kpet] * group))


def _swap_halves(w):
    half = w.shape[-1] // 2
    return jnp.concatenate([w[..., half:], w[..., :half]], axis=-1)


def _pad_lanes(w):
    return jnp.pad(w, [(0, 0)] * (w.ndim - 1) + [(0, LANES - w.shape[-1])])


def _layer_weights(l, dims, w_in, b_gate, q_norm, kv_norm, w_uq, w_uk, w_uv, norm_mix, w_sb_out, w_mla_out, w_mem_out,
                   w_o):
    sbw, q_rank, kv_rank, rope, mem_w = (dims[k] for k in ("sbw", "q_rank", "kv_rank", "rope", "mem_w"))
    heads, nope, vdim = dims["mla_heads"], dims["mla_nope"], dims["mla_v"]
    w = w_in[l]
    edges = [0]
    for size in (sbw, sbw, sbw, q_rank, kv_rank, rope, mem_w):
        edges.append(edges[-1] + size)
    wq, wk, wv, w_cq, w_ckv, w_kpe, w_mq = (w[:, a:b] for a, b in zip(edges[:-1], edges[1:]))
    w_g = w[:, edges[-1]:]
    w_kpe_sw = _swap_halves(w_kpe)
    uq = w_uq[l].reshape(q_rank, heads, nope + rope)
    uq_pe = uq[:, :, nope:]
    uv = w_uv[l]
    uv_pairs = jnp.zeros((heads // 2, 2, kv_rank, 2, vdim), uv.dtype)
    for hh in range(2):
        uv_pairs = uv_pairs.at[:, hh, :, hh, :].set(uv[:, hh::2, :].transpose(1, 0, 2))
    return {
        "norm_mix": norm_mix[l][None, :],
        "w_t": jnp.concatenate([wq, wk, wv, w_kpe, w_kpe_sw], axis=1).T.astype(BF16),
        "w_rest": jnp.concatenate([w_cq, w_ckv, _pad_lanes(w_kpe), _pad_lanes(w_kpe_sw), w_mq], axis=1).astype(BF16),
        "w_g": w_g.astype(BF16),
        "b_g": b_gate[l][None, :],
        "q_norm": q_norm[l][None, :],
        "kv_norm": kv_norm[l][None, :],
        "w_uq_nope": uq[:, :, :nope].reshape(q_rank, heads * nope).astype(BF16),
        "w_uq_pe": _pad_lanes(uq_pe).reshape(q_rank, heads * LANES).astype(BF16),
        "w_uq_pe_sw": _pad_lanes(_swap_halves(uq_pe)).reshape(q_rank, heads * LANES).astype(BF16),
        "w_uk_pairs": w_uk[l].transpose(1, 2, 0).reshape(heads // 2, 2 * nope, kv_rank).astype(BF16),
        "w_uv_pairs": uv_pairs.reshape(heads // 2, 2 * kv_rank, 2 * vdim).astype(BF16),
        "w_sb_out": w_sb_out[l].astype(BF16),
        "w_mla_out": w_mla_out[l].astype(BF16),
        "w_mem_out": w_mem_out[l].astype(BF16),
        "w_o": w_o[l].astype(BF16),
    }


def _pad_page(x, page):
    return jnp.pad(x, [(0, 0)] * (x.ndim - 1) + [(0, page - x.shape[-1])])


def kernel(x_prompt, x_sample, cache_sb_k, cache_sb_v, cache_mla_ckv, cache_mla_kpe, cache_mem_k, cache_mem_v, page_table, mem_prompt, w_in, b_gate, q_norm, kv_norm, w_uq, w_uk, w_uv, mem_norm, w_mk, w_mv, w_sb_out, w_mla_out, w_mem_out, w_o, norm_ffn1, w1_gate, w1_up, w1_down, norm_mix, norm_ffn2, w2_gate, w2_up, w2_down, norm_final):
    nb, seq, d = x_prompt.shape
    db, dt, _ = x_sample.shape
    depth, n_pool, page, sb_heads, sb_d = cache_sb_k.shape
    n_pages = page_table.shape[1]
    mem_tokens, mem_heads, mem_d = cache_mem_k.shape[2:]
    dims = {
        "sbw": sb_heads * sb_d, "sb_d": sb_d, "q_rank": q_norm.shape[1], "kv_rank": cache_mla_ckv.shape[-1],
        "rope": cache_mla_kpe.shape[-1], "mem_w": mem_heads * mem_d, "mem_d": mem_d,
        "mla_heads": w_uk.shape[2], "mla_nope": w_uk.shape[3], "mla_v": w_uv.shape[3],
    }
    sbw, kv_rank, rope, mem_w, heads = (dims[k] for k in ("sbw", "kv_rank", "rope", "mem_w", "mla_heads"))
    pos_p = jnp.arange(seq)
    pos_s = n_pages * page + jnp.arange(dt)
    gf = norm_final[None, :]

    xp = x_prompt.reshape(nb * seq, d)
    xs = x_sample.reshape(db * dt, d)
    outs = [[] for _ in range(10)]
    for l in range(depth):
        ffn1 = (norm_ffn1[l][None, :], w1_gate[l].astype(BF16), w1_up[l].astype(BF16), w1_down[l].astype(BF16), gf)
        ffn2 = (norm_ffn2[l][None, :], w2_gate[l].astype(BF16), w2_up[l].astype(BF16), w2_down[l].astype(BF16), gf)
        w = _layer_weights(l, dims, w_in, b_gate, q_norm, kv_norm, w_uq, w_uk, w_uv, norm_mix, w_sb_out, w_mla_out,
                           w_mem_out, w_o)
        last = l == depth - 1
        xp = _ffn(xp, *ffn1, final=False)
        xs = _ffn(xs, *ffn1, final=False)

        qt, kt, vt, kpet, ckv, kpe, qlat, qpe, mq, gate = _project(xp, nb, pos_p, w, act_dtype=BF16, dims=dims)
        o_sb = _sb_prompt(qt, kt, vt)
        o_lat = _mla_prompt(qlat, qpe, ckv, kpe, nb)
        mk, mv = _mem_kv(mem_prompt.reshape(nb * mem_tokens, d), mem_norm[l][None, :],
                         w_mk[l].reshape(d, mem_w).astype(BF16), w_mv[l].reshape(d, mem_w).astype(BF16))
        o_mem = _mem_attend(mq, mk.reshape(nb, mem_tokens, mem_w), mv.reshape(nb, mem_tokens, mem_w), head_dim=mem_d)
        xp = _merge(xp, gate, o_sb, o_lat, o_mem, w)
        outs[0].append(kt.reshape(nb, sb_heads, sb_d, seq).transpose(0, 3, 1, 2))
        outs[1].append(vt.reshape(nb, sb_heads, sb_d, seq).transpose(0, 3, 1, 2))
        outs[2].append(ckv.reshape(nb, seq, kv_rank))
        outs[3].append(kpet.transpose(0, 2, 1))
        outs[4].append(mk.reshape(nb, mem_tokens, mem_heads, mem_d))
        outs[5].append(mv.reshape(nb, mem_tokens, mem_heads, mem_d))

        qt, kt, vt, kpet, ckv, kpe, qlat, qpe, mq, gate = _project(xs, 1, jnp.tile(pos_s, db), w, act_dtype=F32, dims=dims)
        per_batch = lambda a: a.reshape(a.shape[1], db, dt).transpose(1, 0, 2)
        knew, vnew, pnew = per_batch(kt), per_batch(vt), per_batch(kpet)
        cache_kt = cache_sb_k[l].transpose(0, 2, 3, 1).reshape(n_pool, sbw, page)
        cache_vt = cache_sb_v[l].transpose(0, 2, 3, 1).reshape(n_pool, sbw, page)
        o_sb = _sb_sample(qt[0].T, _pad_page(knew, page), _pad_page(vnew, page), cache_kt, cache_vt, page_table,
                          heads=sb_heads)
        cnew = jnp.pad(ckv.reshape(db, dt, kv_rank), ((0, 0), (0, page - dt), (0, 0)))
        o_lat = _mla_sample(qlat, qpe, cnew, _pad_page(pnew, page), cache_mla_ckv[l],
                            cache_mla_kpe[l].transpose(0, 2, 1), page_table)
        o_mem = _mem_attend(mq, cache_mem_k[l].reshape(db, mem_tokens, mem_w),
                            cache_mem_v[l].reshape(db, mem_tokens, mem_w), head_dim=mem_d)
        xs = _merge(xs, gate, o_sb, o_lat, o_mem, w)
        outs[6].append(knew.reshape(db, sb_heads, sb_d, dt).transpose(0, 3, 1, 2))
        outs[7].append(vnew.reshape(db, sb_heads, sb_d, dt).transpose(0, 3, 1, 2))
        outs[8].append(ckv.reshape(db, dt, kv_rank))
        outs[9].append(pnew.transpose(0, 2, 1))

        xp = _ffn(xp, *ffn2, final=last)
        xs = _ffn(xs, *ffn2, final=last)

    return (xp.reshape(nb, seq, d), xs.reshape(db, dt, d)) + tuple(jnp.stack(o) for o in outs)
```

```python
import functools

import jax
import jax.numpy as jnp
from jax import lax
from jax.experimental import pallas as pl
from jax.experimental.pallas import tpu as pltpu

EPS = 1e-6
ROPE_BASE = 10000.0
LANES = 128
NEG_BIG = -1e30
F32 = jnp.float32
BF16 = jnp.bfloat16
VMEM_LIMIT = 56 * 1024 * 1024

NT_DIMS = (((1,), (1,)), ((), ()))


def _dot(a, b):
    return jnp.dot(a, b, preferred_element_type=F32)


def _dot_nt(a, b):
    return lax.dot_general(a, b, NT_DIMS, preferred_element_type=F32)


def _rms(x, g):
    return x * lax.rsqrt(jnp.mean(x * x, axis=-1, keepdims=True) + EPS) * g


def _const_spec(shape):
    zeros = (0,) * len(shape)
    return pl.BlockSpec(shape, lambda *_: zeros, pipeline_mode=pl.Buffered(1))


def _params(*sem):
    return pltpu.CompilerParams(dimension_semantics=sem, vmem_limit_bytes=VMEM_LIMIT)


def _largest_divisor(n, candidates):
    for c in candidates:
        if n % c == 0:
            return c
    return n


def _lane_tile(x, reps):
    return x if reps == 1 else jnp.concatenate([x] * reps, axis=1)


def _ffn_kernel(x_ref, g_ref, wg_ref, wu_ref, wd_ref, gf_ref, o_ref, *, ff_chunk, final):
    x = x_ref[...]
    h = _rms(x, g_ref[...]).astype(BF16)
    acc = jnp.zeros_like(x)
    for c in range(0, wg_ref.shape[1], ff_chunk):
        a = _dot(h, wg_ref[:, c:c + ff_chunk])
        u = _dot(h, wu_ref[:, c:c + ff_chunk])
        s = (a * jax.nn.sigmoid(a) * u).astype(BF16)
        acc = acc + _dot(s, wd_ref[c:c + ff_chunk, :])
    y = x + 0.5 * acc
    if final:
        y = _rms(y, gf_ref[...])
    o_ref[...] = y


def _ffn(x, g, wg, wu, wd, gf, *, final):
    n, d = x.shape
    f = wg.shape[1]
    tm = _largest_divisor(n, (512, 256, 128, 64, 32, 16, 8))
    kern = functools.partial(_ffn_kernel, ff_chunk=_largest_divisor(f, (512, 256, 128)), final=final)
    return pl.pallas_call(
        kern,
        grid=(n // tm,),
        in_specs=[pl.BlockSpec((tm, d), lambda i: (i, 0)), _const_spec((1, d)), _const_spec((d, f)),
                  _const_spec((d, f)), _const_spec((f, d)), _const_spec((1, d))],
        out_specs=pl.BlockSpec((tm, d), lambda i: (i, 0)),
        out_shape=jax.ShapeDtypeStruct((n, d), F32),
        compiler_params=_params("parallel"),
        name="ffn",
    )(x, g, wg, wu, wd, gf)


def _proj_kernel(x_ref, gmix_ref, wt_ref, wrest_ref, wg_ref, bg_ref, qn_ref, kvn_ref, wuqn_ref, wuqp_ref,
                 wuqs_ref, wuk_ref, cosk_ref, sink_ref, cost_ref, sint_ref,
                 qt_ref, kt_ref, vt_ref, kpet_ref, ckv_ref, kpe_ref, qlat_ref, qpe_ref, mq_ref, gate_ref,
                 *, sbw, rope, q_rank, kv_rank, mem_w, sb_scale, mla_scale, mem_scale, gate_chunk):
    h = _rms(x_ref[...], gmix_ref[...]).astype(BF16)

    t = _dot_nt(wt_ref[...], h)
    qt_ref[0] = t[0:sbw] * sb_scale
    kt_ref[0] = t[sbw:2 * sbw]
    vt_ref[0] = t[2 * sbw:3 * sbw]
    o = 3 * sbw
    kpet_ref[0] = t[o:o + rope] * cost_ref[...] + t[o + rope:o + 2 * rope] * sint_ref[...]

    r = _dot(h, wrest_ref[...])
    cq = r[:, 0:q_rank]
    o = q_rank
    ckv_ref[...] = _rms(r[:, o:o + kv_rank], kvn_ref[...])
    o += kv_rank
    cosk = cosk_ref[...]
    sink = sink_ref[...]
    kpe_ref[...] = (r[:, o:o + LANES] * cosk + r[:, o + LANES:o + 2 * LANES] * sink).astype(kpe_ref.dtype)
    o += 2 * LANES
    mq_ref[...] = r[:, o:o + mem_w] * mem_scale

    cqn = _rms(cq, qn_ref[...]).astype(BF16)
    heads = qlat_ref.shape[0]
    qpe = (_dot(cqn, wuqp_ref[...]) * _lane_tile(cosk, heads)
           + _dot(cqn, wuqs_ref[...]) * _lane_tile(sink, heads)) * mla_scale
    qn = _dot(cqn, wuqn_ref[...])
    nope = LANES // 2
    first = lax.broadcasted_iota(jnp.int32, (qn.shape[0], LANES), 1) < nope
    for hd in range(heads):
        qpe_ref[hd] = qpe[:, hd * LANES:(hd + 1) * LANES].astype(qpe_ref.dtype)
        pair = qn[:, (hd // 2) * LANES:(hd // 2 + 1) * LANES]
        keep = first if hd % 2 == 0 else jnp.logical_not(first)
        lhs = jnp.where(keep, pair, 0.0).astype(BF16)
        qlat_ref[hd] = (_dot(lhs, wuk_ref[hd // 2]) * mla_scale).astype(qlat_ref.dtype)

    for c in range(0, wg_ref.shape[1], gate_chunk):
        g = _dot(h, wg_ref[:, c:c + gate_chunk]) + bg_ref[:, c:c + gate_chunk]
        gate_ref[:, c:c + gate_chunk] = jax.nn.sigmoid(g).astype(gate_ref.dtype)


def _project(x, nb, pos, w, *, act_dtype, dims):
    n, d = x.shape
    s = n // nb
    sbw, rope, q_rank, kv_rank, mem_w, heads = (dims[k] for k in ("sbw", "rope", "q_rank", "kv_rank", "mem_w", "mla_heads"))
    assert sbw % 8 == 0 and rope % 8 == 0 and 2 * dims["mla_nope"] == LANES and heads % 2 == 0
    tm = _largest_divisor(s, (256, 128, 64, 32, 16, 8))
    ns = s // tm

    inv = ROPE_BASE ** (-jnp.arange(0, rope, 2, dtype=F32) / rope)
    ang = pos.astype(F32)[:, None] * inv[None, :]
    cos, sin = jnp.cos(ang), jnp.sin(ang)
    pad = jnp.zeros((s, LANES - rope), F32)
    cosk = jnp.concatenate([cos, cos, pad], axis=1)
    sink = jnp.concatenate([-sin, sin, pad], axis=1)
    cost = jnp.concatenate([cos, cos], axis=1).T
    sint = jnp.concatenate([-sin, sin], axis=1).T

    g3 = w["w_g"].shape[1]
    kern = functools.partial(
        _proj_kernel, sbw=sbw, rope=rope, q_rank=q_rank, kv_rank=kv_rank, mem_w=mem_w,
        sb_scale=dims["sb_d"] ** -0.5, mla_scale=(dims["mla_nope"] + rope) ** -0.5, mem_scale=dims["mem_d"] ** -0.5,
        gate_chunk=_largest_divisor(g3, (512, 256, 128)))
    consts = [w["norm_mix"], w["w_t"], w["w_rest"], w["w_g"], w["b_g"], w["q_norm"], w["kv_norm"],
              w["w_uq_nope"], w["w_uq_pe"], w["w_uq_pe_sw"], w["w_uk_pairs"]]
    tok = lambda b, i: (b * ns + i, 0)
    tok3 = lambda b, i: (0, b * ns + i, 0)
    tr = lambda b, i: (b, 0, i)
    in_specs = ([pl.BlockSpec((tm, d), tok)] + [_const_spec(c.shape) for c in consts]
                + [pl.BlockSpec((tm, LANES), lambda b, i: (i, 0))] * 2
                + [pl.BlockSpec((rope, tm), lambda b, i: (0, i))] * 2)
    out_shape = [jax.ShapeDtypeStruct((nb, sbw, s), F32)] * 3 + [
        jax.ShapeDtypeStruct((nb, rope, s), F32),
        jax.ShapeDtypeStruct((n, kv_rank), F32),
        jax.ShapeDtypeStruct((n, LANES), act_dtype),
        jax.ShapeDtypeStruct((heads, n, kv_rank), act_dtype),
        jax.ShapeDtypeStruct((heads, n, LANES), act_dtype),
        jax.ShapeDtypeStruct((n, mem_w), F32),
        jax.ShapeDtypeStruct((n, g3), BF16)]
    out_specs = [pl.BlockSpec((1, sbw, tm), tr)] * 3 + [
        pl.BlockSpec((1, rope, tm), tr),
        pl.BlockSpec((tm, kv_rank), tok),
        pl.BlockSpec((tm, LANES), tok),
        pl.BlockSpec((heads, tm, kv_rank), tok3),
        pl.BlockSpec((heads, tm, LANES), tok3),
        pl.BlockSpec((tm, mem_w), tok),
        pl.BlockSpec((tm, g3), tok)]
    return pl.pallas_call(
        kern, grid=(nb, ns), in_specs=in_specs, out_specs=out_specs, out_shape=out_shape,
        compiler_params=_params("parallel", "parallel"), name="project",
    )(x, *consts, cosk, sink, cost, sint)


def _sb_weights(z, carry, tcat, valid, newest_first):
    n = z.shape[1] // LANES
    nz = -z
    log_stay_all = jnp.minimum(nz, 0.0) - jnp.log(1.0 + jnp.exp(jnp.minimum(z, nz)))
    log_stay = log_stay_all if valid is None else jnp.where(valid, log_stay_all, 0.0)
    hi = log_stay.astype(BF16)
    lo = (log_stay - hi.astype(F32)).astype(BF16)
    newer = [None] * n
    for g in (range(n) if newest_first else reversed(range(n))):
        sl = slice(g * LANES, (g + 1) * LANES)
        sums = _dot(jnp.concatenate([hi[:, sl], lo[:, sl]], axis=1), tcat)
        newer[g] = sums[:, :LANES] + carry
        carry = carry + sums[:, LANES:]
    w = jnp.exp(z + log_stay_all + jnp.concatenate(newer, axis=1))
    if valid is not None:
        w = jnp.where(valid, w, 0.0)
    return w, carry


def _sb_tcat():
    s = lax.broadcasted_iota(jnp.int32, (LANES, LANES), 0)
    k = lax.broadcasted_iota(jnp.int32, (LANES, LANES), 1)
    half = jnp.concatenate([(s > k).astype(BF16), jnp.ones((LANES, LANES), BF16)], axis=1)
    return jnp.concatenate([half, half], axis=0)


def _causal_pairs(nq, newest_first):
    qi, kb = [], []
    for i in range(nq):
        ks = range(i, -1, -1) if newest_first else range(i + 1)
        qi += [i] * (i + 1)
        kb += list(ks)
    return jnp.asarray(qi, jnp.int32), jnp.asarray(kb, jnp.int32)


def _sb_prompt_kernel(qi_ref, kb_ref, qt_ref, kt_ref, vt_ref, tcat_ref, o_ref, q_sc, carry_sc, acc_sc, *, pairs):
    step_id = pl.program_id(1)
    i = qi_ref[step_id]
    kb = kb_ref[step_id]
    tq = qt_ref.shape[2]
    tk = kt_ref.shape[2]

    @pl.when(kb == i)
    def _():
        first = lax.broadcasted_iota(jnp.int32, (tq, LANES), 1) < LANES // 2
        for p in range(pairs):
            qp = qt_ref[0, p * LANES:(p + 1) * LANES, :].T
            q_sc[2 * p] = jnp.where(first, qp, 0.0).astype(BF16)
            q_sc[2 * p + 1] = jnp.where(first, 0.0, qp).astype(BF16)
        carry_sc[...] = jnp.zeros_like(carry_sc)
        acc_sc[...] = jnp.zeros_like(acc_sc)

    def step(masked):
        valid = None
        if masked:
            valid = (lax.broadcasted_iota(jnp.int32, (tq, tk), 1)
                     < lax.broadcasted_iota(jnp.int32, (tq, tk), 0))
        first = lax.broadcasted_iota(jnp.int32, (tq, LANES), 1) < LANES // 2
        tcat = tcat_ref[...]
        for p in range(pairs):
            kp = kt_ref[0, p * LANES:(p + 1) * LANES, :].astype(BF16)
            vp = vt_ref[0, p * LANES:(p + 1) * LANES, :].astype(BF16)
            pv = []
            for hh in range(2):
                hd = 2 * p + hh
                z = _dot(q_sc[hd], kp)
                w, carry = _sb_weights(z, carry_sc[hd], tcat, valid, newest_first=False)
                carry_sc[hd] = carry
                pv.append(_dot_nt(w.astype(BF16), vp))
            acc_sc[p] += jnp.where(first, pv[0], pv[1])

    pl.when(kb == i)(lambda: step(True))
    pl.when(kb < i)(lambda: step(False))

    @pl.when(kb == 0)
    def _():
        for p in range(pairs):
            o_ref[:, p * LANES:(p + 1) * LANES] = acc_sc[p].astype(o_ref.dtype)


def _sb_prompt(qt, kt, vt):
    nb, sbw, s = qt.shape
    assert sbw % LANES == 0
    pairs = sbw // LANES
    t = _largest_divisor(s, (256, 128))
    nq = s // t
    qi, kb = _causal_pairs(nq, newest_first=True)
    kern = functools.partial(_sb_prompt_kernel, pairs=pairs)
    kv_map = lambda b, p, qi, kb: (b, 0, kb[p])
    grid_spec = pltpu.PrefetchScalarGridSpec(
        num_scalar_prefetch=2, grid=(nb, qi.shape[0]),
        in_specs=[pl.BlockSpec((1, sbw, t), lambda b, p, qi, kb: (b, 0, qi[p])),
                  pl.BlockSpec((1, sbw, t), kv_map), pl.BlockSpec((1, sbw, t), kv_map),
                  _const_spec((2 * LANES, 2 * LANES))],
        out_specs=pl.BlockSpec((t, sbw), lambda b, p, qi, kb: (b * nq + qi[p], 0)),
        scratch_shapes=[pltpu.VMEM((2 * pairs, t, LANES), BF16), pltpu.VMEM((2 * pairs, t, LANES), F32),
                        pltpu.VMEM((pairs, t, LANES), F32)])
    return pl.pallas_call(
        kern, grid_spec=grid_spec, out_shape=jax.ShapeDtypeStruct((nb * s, sbw), BF16),
        compiler_params=_params("parallel", "arbitrary"), name="sb_prompt",
    )(qi, kb, qt, kt, vt, _sb_tcat())


def _softmax_update(s, v, m_sc, l_sc, acc_sc):
    reps_k = s.shape[1] // LANES
    m_prev = m_sc[...]
    m_new = jnp.maximum(m_prev, jnp.max(s, axis=1, keepdims=True))
    p = jnp.exp(s - _lane_tile(m_new, reps_k))
    alpha = jnp.exp(m_prev - m_new)
    l_sc[...] = alpha * l_sc[...] + jnp.sum(p, axis=1, keepdims=True)
    acc_sc[...] = acc_sc[...] * _lane_tile(alpha, acc_sc.shape[1] // LANES) + _dot(p.astype(BF16), v)
    m_sc[...] = m_new


def _mla_prompt_kernel(qi_ref, kb_ref, qlat_ref, qpe_ref, ckv_ref, kpe_ref, o_ref, m_sc, l_sc, acc_sc):
    step_id = pl.program_id(1)
    i = qi_ref[step_id]
    j = kb_ref[step_id]
    heads, tq, c = qlat_ref.shape
    tk = ckv_ref.shape[0]

    @pl.when(j == 0)
    def _():
        m_sc[...] = jnp.full_like(m_sc, NEG_BIG)
        l_sc[...] = jnp.zeros_like(l_sc)
        acc_sc[...] = jnp.zeros_like(acc_sc)

    def step(masked):
        ckv = ckv_ref[...].astype(BF16)
        s = (_dot_nt(qlat_ref[...].reshape(heads * tq, c), ckv)
             + _dot_nt(qpe_ref[...].reshape(heads * tq, LANES), kpe_ref[...]))
        if masked:
            s = s.reshape(heads, tq, tk)
            key = lax.broadcasted_iota(jnp.int32, (heads, tq, tk), 2)
            qry = lax.broadcasted_iota(jnp.int32, (heads, tq, tk), 1)
            s = jnp.where(key <= qry, s, -jnp.inf).reshape(heads * tq, tk)
        _softmax_update(s, ckv, m_sc, l_sc, acc_sc)

    pl.when(j < i)(lambda: step(False))
    pl.when(j == i)(lambda: step(True))

    @pl.when(j == i)
    def _():
        o = acc_sc[...] / _lane_tile(l_sc[...], c // LANES)
        o_ref[...] = o.reshape(heads, tq, c).astype(o_ref.dtype)


def _mla_prompt(qlat, qpe, ckv, kpe, nb):
    heads, n, c = qlat.shape
    assert c % LANES == 0
    s = n // nb
    t = _largest_divisor(s, (256, 128))
    nq = s // t
    qi, kb = _causal_pairs(nq, newest_first=False)
    q_map = lambda b, p, qi, kb: (0, b * nq + qi[p], 0)
    k_map = lambda b, p, qi, kb: (b * nq + kb[p], 0)
    grid_spec = pltpu.PrefetchScalarGridSpec(
        num_scalar_prefetch=2, grid=(nb, qi.shape[0]),
        in_specs=[pl.BlockSpec((heads, t, c), q_map), pl.BlockSpec((heads, t, LANES), q_map),
                  pl.BlockSpec((t, c), k_map), pl.BlockSpec((t, LANES), k_map)],
        out_specs=pl.BlockSpec((heads, t, c), q_map),
        scratch_shapes=[pltpu.VMEM((heads * t, LANES), F32), pltpu.VMEM((heads * t, LANES), F32),
                        pltpu.VMEM((heads * t, c), F32)])
    return pl.pallas_call(
        _mla_prompt_kernel, grid_spec=grid_spec, out_shape=jax.ShapeDtypeStruct((heads, n, c), BF16),
        compiler_params=_params("parallel", "arbitrary"), name="mla_prompt",
    )(qi, kb, qlat, qpe, ckv, kpe)


def _mem_kv_kernel(mem_ref, g_ref, wk_ref, wv_ref, k_ref, v_ref):
    mn = _rms(mem_ref[...], g_ref[...]).astype(BF16)
    k_ref[...] = _dot(mn, wk_ref[...])
    v_ref[...] = _dot(mn, wv_ref[...])


def _mem_kv(mem, g, wk, wv):
    n, d = mem.shape
    mw = wk.shape[1]
    tm = _largest_divisor(n, (512, 256, 128, 64, 32, 16, 8))
    row = lambda i: (i, 0)
    return pl.pallas_call(
        _mem_kv_kernel, grid=(n // tm,),
        in_specs=[pl.BlockSpec((tm, d), row), _const_spec((1, d)), _const_spec((d, mw)), _const_spec((d, mw))],
        out_specs=[pl.BlockSpec((tm, mw), row)] * 2,
        out_shape=[jax.ShapeDtypeStruct((n, mw), F32)] * 2,
        compiler_params=_params("parallel"), name="mem_kv",
    )(mem, g, wk, wv)


def _mem_attend_kernel(q_ref, k_ref, v_ref, o_ref, *, head_dim, interleaved):
    heads = q_ref.shape[1] // head_dim
    for h in range(heads):
        c = h * head_dim
        if interleaved:
            rows = pl.ds(h, k_ref.shape[1] // heads, stride=heads)
            k, v = k_ref[0, rows, :], v_ref[0, rows, :]
        else:
            k, v = k_ref[0, :, c:c + head_dim], v_ref[0, :, c:c + head_dim]
        s = _dot_nt(q_ref[:, c:c + head_dim].astype(BF16), k.astype(BF16))
        p = jnp.exp(s - jnp.max(s, axis=1, keepdims=True))
        o = _dot(p.astype(BF16), v.astype(BF16))
        o_ref[:, c:c + head_dim] = o / jnp.sum(p, axis=1, keepdims=True)


def _mem_attend(mq, mk, mv, *, head_dim, interleaved):
    n, mw = mq.shape
    nb, m, kw = mk.shape
    assert head_dim == LANES and kw == (head_dim if interleaved else mw)
    t = n // nb
    tq = _largest_divisor(t, (512, 256, 128, 64, 32, 16, 8))
    nt = t // tq
    kern = functools.partial(_mem_attend_kernel, head_dim=head_dim, interleaved=interleaved)
    return pl.pallas_call(
        kern, grid=(nb, nt),
        in_specs=[pl.BlockSpec((tq, mw), lambda b, i: (b * nt + i, 0)),
                  pl.BlockSpec((1, m, kw), lambda b, i: (b, 0, 0)), pl.BlockSpec((1, m, kw), lambda b, i: (b, 0, 0))],
        out_specs=pl.BlockSpec((tq, mw), lambda b, i: (b * nt + i, 0)),
        out_shape=jax.ShapeDtypeStruct((n, mw), F32),
        compiler_params=_params("parallel", "parallel"), name="mem_attend",
    )(mq, mk, mv)


def _merge_kernel(x_ref, gate_ref, osb_ref, olat_ref, omem_ref, wuv_ref, wsb_ref, wmla_ref, wmem_ref, wo_ref, o_ref):
    d = x_ref.shape[1]
    heads = olat_ref.shape[0]
    parts = []
    for p in range(heads // 2):
        lhs = jnp.concatenate([olat_ref[2 * p].astype(BF16), olat_ref[2 * p + 1].astype(BF16)], axis=1)
        parts.append(_dot(lhs, wuv_ref[p]))
    o_mla = jnp.concatenate(parts, axis=1).astype(BF16)
    m = (gate_ref[:, 0:d].astype(F32) * _dot(osb_ref[...].astype(BF16), wsb_ref[...])
         + gate_ref[:, d:2 * d].astype(F32) * _dot(o_mla, wmla_ref[...])
         + gate_ref[:, 2 * d:3 * d].astype(F32) * _dot(omem_ref[...].astype(BF16), wmem_ref[...]))
    o_ref[...] = x_ref[...] + _dot(m.astype(BF16), wo_ref[...])


def _merge(x, gate, o_sb, o_lat, o_mem, w):
    n, d = x.shape
    heads, _, c = o_lat.shape
    tm = _largest_divisor(n, (512, 256, 128, 64, 32, 16, 8))
    row = lambda i: (i, 0)
    consts = [w["w_uv_pairs"], w["w_sb_out"], w["w_mla_out"], w["w_mem_out"], w["w_o"]]
    return pl.pallas_call(
        _merge_kernel, grid=(n // tm,),
        in_specs=[pl.BlockSpec((tm, d), row), pl.BlockSpec((tm, gate.shape[1]), row),
                  pl.BlockSpec((tm, o_sb.shape[1]), row), pl.BlockSpec((heads, tm, c), lambda i: (0, i, 0)),
                  pl.BlockSpec((tm, o_mem.shape[1]), row)] + [_const_spec(a.shape) for a in consts],
        out_specs=pl.BlockSpec((tm, d), row),
        out_shape=jax.ShapeDtypeStruct((n, d), F32),
        compiler_params=_params("parallel"), name="merge",
    )(x, gate, o_sb, o_lat, o_mem, *consts)


def _sb_sample_kernel(pt_ref, q_ref, knew_ref, vnew_ref, tcat_ref, *rest, group, heads):
    k_refs = rest[:group]
    v_refs = rest[group:2 * group]
    o_ref, qbd_sc, carry_sc, acc_sc = rest[2 * group:]
    s = pl.program_id(1)
    t, sbw = q_ref.shape
    rows = heads * t
    hd = sbw // heads
    tk = knew_ref.shape[2]
    tcat = tcat_ref[...]

    def own_head():
        row = lax.broadcasted_iota(jnp.int32, (rows, sbw), 0)
        col = lax.broadcasted_iota(jnp.int32, (rows, sbw), 1)
        mask = None
        for h in range(heads):
            m = (row >= h * t) & (row < (h + 1) * t) & (col >= h * hd) & (col < (h + 1) * hd)
            mask = m if mask is None else mask | m
        return mask

    def block(ks, vs, valid):
        kcat = jnp.concatenate([k.astype(BF16) for k in ks], axis=1)
        z = _dot(qbd_sc[...], kcat)
        w, carry = _sb_weights(z, carry_sc[...], tcat, valid, newest_first=True)
        carry_sc[...] = carry
        vcat = jnp.concatenate([v.astype(BF16) for v in vs], axis=1)
        acc_sc[...] += _dot_nt(w.astype(BF16), vcat)

    @pl.when(s == 0)
    def _():
        q = jnp.concatenate([q_ref[...]] * heads, axis=0)
        qbd_sc[...] = jnp.where(own_head(), q, 0.0).astype(BF16)
        carry_sc[...] = jnp.zeros_like(carry_sc)
        acc_sc[...] = jnp.zeros_like(acc_sc)
        key = lax.broadcasted_iota(jnp.int32, (rows, tk), 1)
        qry = lax.rem(lax.broadcasted_iota(jnp.int32, (rows, tk), 0), t)
        block([knew_ref[0]], [vnew_ref[0]], key < qry)

    block([r[...] for r in k_refs], [r[...] for r in v_refs], None)

    @pl.when(s == pl.num_programs(1) - 1)
    def _():
        diag = jnp.where(own_head(), acc_sc[...], 0.0).reshape(heads, t, sbw)
        o_ref[...] = jnp.sum(diag, axis=0)


def _sb_sample(q, knew, vnew, cache_k, cache_v, page_table, *, heads):
    n, sbw = q.shape
    nb, n_pages = page_table.shape
    t = n // nb
    page = cache_k.shape[2]
    assert page == LANES
    group = _largest_divisor(n_pages, (16, 8, 4, 2, 1))
    steps = n_pages // group

    def page_spec(g):
        def index(b, s, pt):
            return (pt[b * n_pages + n_pages - 1 - (s * group + g)], 0, 0)
        return pl.BlockSpec((None, sbw, page), index)

    own = lambda b, s, pt: (b, 0)
    new = lambda b, s, pt: (b, 0, 0)
    kern = functools.partial(_sb_sample_kernel, group=group, heads=heads)
    grid_spec = pltpu.PrefetchScalarGridSpec(
        num_scalar_prefetch=1, grid=(nb, steps),
        in_specs=[pl.BlockSpec((t, sbw), own), pl.BlockSpec((1, sbw, page), new), pl.BlockSpec((1, sbw, page), new),
                  pl.BlockSpec((2 * LANES, 2 * LANES), lambda b, s, pt: (0, 0))]
        + [page_spec(g) for g in range(group)] * 2,
        out_specs=pl.BlockSpec((t, sbw), own),
        scratch_shapes=[pltpu.VMEM((heads * t, sbw), BF16), pltpu.VMEM((heads * t, LANES), F32),
                        pltpu.VMEM((heads * t, sbw), F32)])
    return pl.pallas_call(
        kern, grid_spec=grid_spec, out_shape=jax.ShapeDtypeStruct((n, sbw), F32),
        compiler_params=_params("parallel", "arbitrary"), name="sb_sample",
    )(page_table.reshape(-1), q, knew, vnew, _sb_tcat(), *([cache_k] * group), *([cache_v] * group))


def _mla_sample_kernel(pt_ref, qlat_ref, qpe_ref, cnew_ref, pnew_ref, *rest, group):
    c_refs = rest[:group]
    p_refs = rest[group:2 * group]
    o_ref, m_sc, l_sc, acc_sc = rest[2 * group:]
    s = pl.program_id(1)
    heads, t, c = qlat_ref.shape
    rows = heads * t
    rope, tk = pnew_ref.shape[1:]
    qlat = qlat_ref[...].reshape(rows, c).astype(BF16)
    qpe = qpe_ref[...].reshape(rows, LANES).astype(BF16)

    def scores(ckv, kpets):
        kpe = jnp.concatenate([p.astype(BF16) for p in kpets], axis=1)
        kpe = jnp.concatenate([kpe, jnp.zeros((LANES - rope, kpe.shape[1]), BF16)], axis=0)
        return _dot_nt(qlat, ckv) + _dot(qpe, kpe)

    @pl.when(s == 0)
    def _():
        m_sc[...] = jnp.full_like(m_sc, NEG_BIG)
        l_sc[...] = jnp.zeros_like(l_sc)
        acc_sc[...] = jnp.zeros_like(acc_sc)
        ckv = cnew_ref[0].astype(BF16)
        key = lax.broadcasted_iota(jnp.int32, (rows, tk), 1)
        qry = lax.rem(lax.broadcasted_iota(jnp.int32, (rows, tk), 0), t)
        sc = jnp.where(key <= qry, scores(ckv, [pnew_ref[0]]), -jnp.inf)
        _softmax_update(sc, ckv, m_sc, l_sc, acc_sc)

    ckv = jnp.concatenate([r[...].astype(BF16) for r in c_refs], axis=0)
    _softmax_update(scores(ckv, [r[...] for r in p_refs]), ckv, m_sc, l_sc, acc_sc)

    @pl.when(s == pl.num_programs(1) - 1)
    def _():
        o = acc_sc[...] / _lane_tile(l_sc[...], c // LANES)
        o_ref[...] = o.reshape(heads, t, c)


def _mla_sample(qlat, qpe, cnew, pnew, cache_ckv, cache_kpet, page_table):
    heads, n, c = qlat.shape
    nb, n_pages = page_table.shape
    t = n // nb
    page = cache_ckv.shape[1]
    rope = cache_kpet.shape[1]
    assert page == LANES
    group = _largest_divisor(n_pages, (32, 16, 8, 4, 2, 1))
    steps = n_pages // group

    def page_spec(shape, g):
        def index(b, s, pt):
            return (pt[b * n_pages + s * group + g], 0, 0)
        return pl.BlockSpec((None,) + shape, index)

    own = lambda b, s, pt: (0, b, 0)
    new = lambda b, s, pt: (b, 0, 0)
    kern = functools.partial(_mla_sample_kernel, group=group)
    grid_spec = pltpu.PrefetchScalarGridSpec(
        num_scalar_prefetch=1, grid=(nb, steps),
        in_specs=[pl.BlockSpec((heads, t, c), own), pl.BlockSpec((heads, t, LANES), own),
                  pl.BlockSpec((1, page, c), new), pl.BlockSpec((1, rope, page), new)]
        + [page_spec((page, c), g) for g in range(group)] + [page_spec((rope, page), g) for g in range(group)],
        out_specs=pl.BlockSpec((heads, t, c), own),
        scratch_shapes=[pltpu.VMEM((heads * t, LANES), F32), pltpu.VMEM((heads * t, LANES), F32),
                        pltpu.VMEM((heads * t, c), F32)])
    return pl.pallas_call(
        kern, grid_spec=grid_spec, out_shape=jax.ShapeDtypeStruct((heads, n, c), F32),
        compiler_params=_params("parallel", "arbitrary"), name="mla_sample",
    )(page_table.reshape(-1), qlat, qpe, cnew, pnew, *([cache_ckv] * group), *([cache_kpet] * group))


def _swap_halves(w):
    half = w.shape[-1] // 2
    return jnp.concatenate([w[..., half:], w[..., :half]], axis=-1)


def _pad_lanes(w):
    return jnp.pad(w, [(0, 0)] * (w.ndim - 1) + [(0, LANES - w.shape[-1])])


def _layer_weights(l, dims, w_in, b_gate, q_norm, kv_norm, w_uq, w_uk, w_uv, norm_mix, w_sb_out, w_mla_out, w_mem_out,
                   w_o):
    sbw, q_rank, kv_rank, rope, mem_w = (dims[k] for k in ("sbw", "q_rank", "kv_rank", "rope", "mem_w"))
    heads, nope, vdim = dims["mla_heads"], dims["mla_nope"], dims["mla_v"]
    w = w_in[l]
    edges = [0]
    for size in (sbw, sbw, sbw, q_rank, kv_rank, rope, mem_w):
        edges.append(edges[-1] + size)
    wq, wk, wv, w_cq, w_ckv, w_kpe, w_mq = (w[:, a:b] for a, b in zip(edges[:-1], edges[1:]))
    w_g = w[:, edges[-1]:]
    w_kpe_sw = _swap_halves(w_kpe)
    uq = w_uq[l].reshape(q_rank, heads, nope + rope)
    uq_pe = uq[:, :, nope:]
    uv = w_uv[l]
    uv_pairs = jnp.zeros((heads // 2, 2, kv_rank, 2, vdim), uv.dtype)
    for hh in range(2):
        uv_pairs = uv_pairs.at[:, hh, :, hh, :].set(uv[:, hh::2, :].transpose(1, 0, 2))
    return {
        "norm_mix": norm_mix[l][None, :],
        "w_t": jnp.concatenate([wq, wk, wv, w_kpe, w_kpe_sw], axis=1).T.astype(BF16),
        "w_rest": jnp.concatenate([w_cq, w_ckv, _pad_lanes(w_kpe), _pad_lanes(w_kpe_sw), w_mq], axis=1).astype(BF16),
        "w_g": w_g.astype(BF16),
        "b_g": b_gate[l][None, :],
        "q_norm": q_norm[l][None, :],
        "kv_norm": kv_norm[l][None, :],
        "w_uq_nope": uq[:, :, :nope].reshape(q_rank, heads * nope).astype(BF16),
        "w_uq_pe": _pad_lanes(uq_pe).reshape(q_rank, heads * LANES).astype(BF16),
        "w_uq_pe_sw": _pad_lanes(_swap_halves(uq_pe)).reshape(q_rank, heads * LANES).astype(BF16),
        "w_uk_pairs": w_uk[l].transpose(1, 2, 0).reshape(heads // 2, 2 * nope, kv_rank).astype(BF16),
        "w_uv_pairs": uv_pairs.reshape(heads // 2, 2 * kv_rank, 2 * vdim).astype(BF16),
        "w_sb_out": w_sb_out[l].astype(BF16),
        "w_mla_out": w_mla_out[l].astype(BF16),
        "w_mem_out": w_mem_out[l].astype(BF16),
        "w_o": w_o[l].astype(BF16),
    }


def _pad_page(x, page):
    return jnp.pad(x, [(0, 0)] * (x.ndim - 1) + [(0, page - x.shape[-1])])


def kernel(x_prompt, x_sample, cache_sb_k, cache_sb_v, cache_mla_ckv, cache_mla_kpe, cache_mem_k, cache_mem_v, page_table, mem_prompt, w_in, b_gate, q_norm, kv_norm, w_uq, w_uk, w_uv, mem_norm, w_mk, w_mv, w_sb_out, w_mla_out, w_mem_out, w_o, norm_ffn1, w1_gate, w1_up, w1_down, norm_mix, norm_ffn2, w2_gate, w2_up, w2_down, norm_final):
    nb, seq, d = x_prompt.shape
    db, dt, _ = x_sample.shape
    depth, n_pool, page, sb_heads, sb_d = cache_sb_k.shape
    n_pages = page_table.shape[1]
    mem_tokens, mem_heads, mem_d = cache_mem_k.shape[2:]
    dims = {
        "sbw": sb_heads * sb_d, "sb_d": sb_d, "q_rank": q_norm.shape[1], "kv_rank": cache_mla_ckv.shape[-1],
        "rope": cache_mla_kpe.shape[-1], "mem_w": mem_heads * mem_d, "mem_d": mem_d,
        "mla_heads": w_uk.shape[2], "mla_nope": w_uk.shape[3], "mla_v": w_uv.shape[3],
    }
    sbw, kv_rank, rope, mem_w, heads = (dims[k] for k in ("sbw", "kv_rank", "rope", "mem_w", "mla_heads"))
    pos_p = jnp.arange(seq)
    pos_s = n_pages * page + jnp.arange(dt)
    gf = norm_final[None, :]

    xp = x_prompt.reshape(nb * seq, d)
    xs = x_sample.reshape(db * dt, d)
    outs = [[] for _ in range(10)]
    for l in range(depth):
        ffn1 = (norm_ffn1[l][None, :], w1_gate[l].astype(BF16), w1_up[l].astype(BF16), w1_down[l].astype(BF16), gf)
        ffn2 = (norm_ffn2[l][None, :], w2_gate[l].astype(BF16), w2_up[l].astype(BF16), w2_down[l].astype(BF16), gf)
        w = _layer_weights(l, dims, w_in, b_gate, q_norm, kv_norm, w_uq, w_uk, w_uv, norm_mix, w_sb_out, w_mla_out,
                           w_mem_out, w_o)
        last = l == depth - 1
        xp = _ffn(xp, *ffn1, final=False)
        xs = _ffn(xs, *ffn1, final=False)

        qt, kt, vt, kpet, ckv, kpe, qlat, qpe, mq, gate = _project(xp, nb, pos_p, w, act_dtype=BF16, dims=dims)
        o_sb = _sb_prompt(qt, kt, vt)
        o_lat = _mla_prompt(qlat, qpe, ckv, kpe, nb)
        mk, mv = _mem_kv(mem_prompt.reshape(nb * mem_tokens, d), mem_norm[l][None, :],
                         w_mk[l].reshape(d, mem_w).astype(BF16), w_mv[l].reshape(d, mem_w).astype(BF16))
        o_mem = _mem_attend(mq, mk.reshape(nb, mem_tokens, mem_w), mv.reshape(nb, mem_tokens, mem_w), head_dim=mem_d,
                            interleaved=False)
        xp = _merge(xp, gate, o_sb, o_lat, o_mem, w)
        outs[0].append(kt.reshape(nb, sb_heads, sb_d, seq).transpose(0, 3, 1, 2))
        outs[1].append(vt.reshape(nb, sb_heads, sb_d, seq).transpose(0, 3, 1, 2))
        outs[2].append(ckv.reshape(nb, seq, kv_rank))
        outs[3].append(kpet.transpose(0, 2, 1))
        outs[4].append(mk.reshape(nb, mem_tokens, mem_heads, mem_d))
        outs[5].append(mv.reshape(nb, mem_tokens, mem_heads, mem_d))

        qt, kt, vt, kpet, ckv, kpe, qlat, qpe, mq, gate = _project(xs, 1, jnp.tile(pos_s, db), w, act_dtype=F32, dims=dims)
        per_batch = lambda a: a.reshape(a.shape[1], db, dt).transpose(1, 0, 2)
        knew, vnew, pnew = per_batch(kt), per_batch(vt), per_batch(kpet)
        cache_kt = cache_sb_k[l].transpose(0, 2, 3, 1).reshape(n_pool, sbw, page)
        cache_vt = cache_sb_v[l].transpose(0, 2, 3, 1).reshape(n_pool, sbw, page)
        o_sb = _sb_sample(qt[0].T, _pad_page(knew, page), _pad_page(vnew, page), cache_kt, cache_vt, page_table,
                          heads=sb_heads)
        cnew = jnp.pad(ckv.reshape(db, dt, kv_rank), ((0, 0), (0, page - dt), (0, 0)))
        o_lat = _mla_sample(qlat, qpe, cnew, _pad_page(pnew, page), cache_mla_ckv[l],
                            cache_mla_kpe[l].transpose(0, 2, 1), page_table)
        o_mem = _mem_attend(mq, cache_mem_k[l].reshape(db, mem_tokens * mem_heads, mem_d),
                            cache_mem_v[l].reshape(db, mem_tokens * mem_heads, mem_d), head_dim=mem_d, interleaved=True)
        xs = _merge(xs, gate, o_sb, o_lat, o_mem, w)
        outs[6].append(knew.reshape(db, sb_heads, sb_d, dt).transpose(0, 3, 1, 2))
        outs[7].append(vnew.reshape(db, sb_heads, sb_d, dt).transpose(0, 3, 1, 2))
        outs[8].append(ckv.reshape(db, dt, kv_rank))
        outs[9].append(pnew.transpose(0, 2, 1))

        xp = _ffn(xp, *ffn2, final=last)
        xs = _ffn(xs, *ffn2, final=last)

    return (xp.reshape(nb, seq, d), xs.reshape(db, dt, d)) + tuple(jnp.stack(o) for o in outs)
```

```python
import functools

import jax
import jax.numpy as jnp
from jax import lax
from jax.experimental import pallas as pl
from jax.experimental.pallas import tpu as pltpu

EPS = 1e-6
ROPE_BASE = 10000.0
LANES = 128
NEG_BIG = -1e30
F32 = jnp.float32
BF16 = jnp.bfloat16
VMEM_LIMIT = 56 * 1024 * 1024

NT_DIMS = (((1,), (1,)), ((), ()))


def _dot(a, b):
    return jnp.dot(a, b, preferred_element_type=F32)


def _dot_nt(a, b):
    return lax.dot_general(a, b, NT_DIMS, preferred_element_type=F32)


def _rms(x, g):
    return x * lax.rsqrt(jnp.mean(x * x, axis=-1, keepdims=True) + EPS) * g


def _const_spec(shape):
    zeros = (0,) * len(shape)
    return pl.BlockSpec(shape, lambda *_: zeros, pipeline_mode=pl.Buffered(1))


def _params(*sem):
    return pltpu.CompilerParams(dimension_semantics=sem, vmem_limit_bytes=VMEM_LIMIT)


def _largest_divisor(n, candidates):
    for c in candidates:
        if n % c == 0:
            return c
    return n


def _lane_tile(x, reps):
    return x if reps == 1 else jnp.concatenate([x] * reps, axis=1)


def _ffn_kernel(x_ref, g_ref, wg_ref, wu_ref, wd_ref, gf_ref, o_ref, *, ff_chunk, final):
    x = x_ref[...]
    h = _rms(x, g_ref[...]).astype(BF16)
    acc = jnp.zeros_like(x)
    for c in range(0, wg_ref.shape[1], ff_chunk):
        a = _dot(h, wg_ref[:, c:c + ff_chunk])
        u = _dot(h, wu_ref[:, c:c + ff_chunk])
        s = (a * jax.nn.sigmoid(a) * u).astype(BF16)
        acc = acc + _dot(s, wd_ref[c:c + ff_chunk, :])
    y = x + 0.5 * acc
    if final:
        y = _rms(y, gf_ref[...])
    o_ref[...] = y


def _ffn(x, g, wg, wu, wd, gf, *, final):
    n, d = x.shape
    f = wg.shape[1]
    tm = _largest_divisor(n, (512, 256, 128, 64, 32, 16, 8))
    kern = functools.partial(_ffn_kernel, ff_chunk=_largest_divisor(f, (512, 256, 128)), final=final)
    return pl.pallas_call(
        kern,
        grid=(n // tm,),
        in_specs=[pl.BlockSpec((tm, d), lambda i: (i, 0)), _const_spec((1, d)), _const_spec((d, f)),
                  _const_spec((d, f)), _const_spec((f, d)), _const_spec((1, d))],
        out_specs=pl.BlockSpec((tm, d), lambda i: (i, 0)),
        out_shape=jax.ShapeDtypeStruct((n, d), F32),
        compiler_params=_params("parallel"),
        name="ffn",
    )(x, g, wg, wu, wd, gf)


def _proj_kernel(x_ref, gmix_ref, wt_ref, wrest_ref, wg_ref, bg_ref, qn_ref, kvn_ref, wuqn_ref, wuqp_ref,
                 wuqs_ref, wuk_ref, cosk_ref, sink_ref, cost_ref, sint_ref,
                 qt_ref, kt_ref, vt_ref, kpet_ref, ckv_ref, kpe_ref, qlat_ref, qpe_ref, mq_ref, gate_ref,
                 *, sbw, rope, q_rank, kv_rank, mem_w, sb_scale, mla_scale, mem_scale, gate_chunk):
    h = _rms(x_ref[...], gmix_ref[...]).astype(BF16)

    t = _dot_nt(wt_ref[...], h)
    qt_ref[0] = t[0:sbw] * sb_scale
    kt_ref[0] = t[sbw:2 * sbw]
    vt_ref[0] = t[2 * sbw:3 * sbw]
    o = 3 * sbw
    kpet_ref[0] = t[o:o + rope] * cost_ref[...] + t[o + rope:o + 2 * rope] * sint_ref[...]

    r = _dot(h, wrest_ref[...])
    cq = r[:, 0:q_rank]
    o = q_rank
    ckv_ref[...] = _rms(r[:, o:o + kv_rank], kvn_ref[...])
    o += kv_rank
    cosk = cosk_ref[...]
    sink = sink_ref[...]
    kpe_ref[...] = (r[:, o:o + LANES] * cosk + r[:, o + LANES:o + 2 * LANES] * sink).astype(kpe_ref.dtype)
    o += 2 * LANES
    mq_ref[...] = r[:, o:o + mem_w] * mem_scale

    cqn = _rms(cq, qn_ref[...]).astype(BF16)
    heads = qlat_ref.shape[0]
    qpe = (_dot(cqn, wuqp_ref[...]) * _lane_tile(cosk, heads)
           + _dot(cqn, wuqs_ref[...]) * _lane_tile(sink, heads)) * mla_scale
    qn = _dot(cqn, wuqn_ref[...])
    nope = LANES // 2
    first = lax.broadcasted_iota(jnp.int32, (qn.shape[0], LANES), 1) < nope
    for hd in range(heads):
        qpe_ref[hd] = qpe[:, hd * LANES:(hd + 1) * LANES].astype(qpe_ref.dtype)
        pair = qn[:, (hd // 2) * LANES:(hd // 2 + 1) * LANES]
        keep = first if hd % 2 == 0 else jnp.logical_not(first)
        lhs = jnp.where(keep, pair, 0.0).astype(BF16)
        qlat_ref[hd] = (_dot(lhs, wuk_ref[hd // 2]) * mla_scale).astype(qlat_ref.dtype)

    for c in range(0, wg_ref.shape[1], gate_chunk):
        g = _dot(h, wg_ref[:, c:c + gate_chunk]) + bg_ref[:, c:c + gate_chunk]
        gate_ref[:, c:c + gate_chunk] = jax.nn.sigmoid(g).astype(gate_ref.dtype)


def _project(x, nb, pos, w, *, act_dtype, dims):
    n, d = x.shape
    s = n // nb
    sbw, rope, q_rank, kv_rank, mem_w, heads = (dims[k] for k in ("sbw", "rope", "q_rank", "kv_rank", "mem_w", "mla_heads"))
    assert sbw % 8 == 0 and rope % 8 == 0 and 2 * dims["mla_nope"] == LANES and heads % 2 == 0
    tm = _largest_divisor(s, (256, 128, 64, 32, 16, 8))
    ns = s // tm

    inv = ROPE_BASE ** (-jnp.arange(0, rope, 2, dtype=F32) / rope)
    ang = pos.astype(F32)[:, None] * inv[None, :]
    cos, sin = jnp.cos(ang), jnp.sin(ang)
    pad = jnp.zeros((s, LANES - rope), F32)
    cosk = jnp.concatenate([cos, cos, pad], axis=1)
    sink = jnp.concatenate([-sin, sin, pad], axis=1)
    cost = jnp.concatenate([cos, cos], axis=1).T
    sint = jnp.concatenate([-sin, sin], axis=1).T

    g3 = w["w_g"].shape[1]
    kern = functools.partial(
        _proj_kernel, sbw=sbw, rope=rope, q_rank=q_rank, kv_rank=kv_rank, mem_w=mem_w,
        sb_scale=dims["sb_d"] ** -0.5, mla_scale=(dims["mla_nope"] + rope) ** -0.5, mem_scale=dims["mem_d"] ** -0.5,
        gate_chunk=_largest_divisor(g3, (512, 256, 128)))
    consts = [w["norm_mix"], w["w_t"], w["w_rest"], w["w_g"], w["b_g"], w["q_norm"], w["kv_norm"],
              w["w_uq_nope"], w["w_uq_pe"], w["w_uq_pe_sw"], w["w_uk_pairs"]]
    tok = lambda b, i: (b * ns + i, 0)
    tok3 = lambda b, i: (0, b * ns + i, 0)
    tr = lambda b, i: (b, 0, i)
    in_specs = ([pl.BlockSpec((tm, d), tok)] + [_const_spec(c.shape) for c in consts]
                + [pl.BlockSpec((tm, LANES), lambda b, i: (i, 0))] * 2
                + [pl.BlockSpec((rope, tm), lambda b, i: (0, i))] * 2)
    out_shape = [jax.ShapeDtypeStruct((nb, sbw, s), F32)] * 3 + [
        jax.ShapeDtypeStruct((nb, rope, s), F32),
        jax.ShapeDtypeStruct((n, kv_rank), F32),
        jax.ShapeDtypeStruct((n, LANES), act_dtype),
        jax.ShapeDtypeStruct((heads, n, kv_rank), act_dtype),
        jax.ShapeDtypeStruct((heads, n, LANES), act_dtype),
        jax.ShapeDtypeStruct((n, mem_w), F32),
        jax.ShapeDtypeStruct((n, g3), BF16)]
    out_specs = [pl.BlockSpec((1, sbw, tm), tr)] * 3 + [
        pl.BlockSpec((1, rope, tm), tr),
        pl.BlockSpec((tm, kv_rank), tok),
        pl.BlockSpec((tm, LANES), tok),
        pl.BlockSpec((heads, tm, kv_rank), tok3),
        pl.BlockSpec((heads, tm, LANES), tok3),
        pl.BlockSpec((tm, mem_w), tok),
        pl.BlockSpec((tm, g3), tok)]
    return pl.pallas_call(
        kern, grid=(nb, ns), in_specs=in_specs, out_specs=out_specs, out_shape=out_shape,
        compiler_params=_params("parallel", "parallel"), name="project",
    )(x, *consts, cosk, sink, cost, sint)


def _sb_weights(z, carry, tcat, valid, newest_first):
    n = z.shape[1] // LANES
    nz = -z
    log_stay_all = jnp.minimum(nz, 0.0) - jnp.log(1.0 + jnp.exp(jnp.minimum(z, nz)))
    log_stay = log_stay_all if valid is None else jnp.where(valid, log_stay_all, 0.0)
    hi = log_stay.astype(BF16)
    lo = (log_stay - hi.astype(F32)).astype(BF16)
    newer = [None] * n
    for g in (range(n) if newest_first else reversed(range(n))):
        sl = slice(g * LANES, (g + 1) * LANES)
        sums = _dot(jnp.concatenate([hi[:, sl], lo[:, sl]], axis=1), tcat)
        newer[g] = sums[:, :LANES] + carry
        carry = carry + sums[:, LANES:]
    w = jnp.exp(z + log_stay_all + jnp.concatenate(newer, axis=1))
    if valid is not None:
        w = jnp.where(valid, w, 0.0)
    return w, carry


def _sb_tcat():
    s = lax.broadcasted_iota(jnp.int32, (LANES, LANES), 0)
    k = lax.broadcasted_iota(jnp.int32, (LANES, LANES), 1)
    half = jnp.concatenate([(s > k).astype(BF16), jnp.ones((LANES, LANES), BF16)], axis=1)
    return jnp.concatenate([half, half], axis=0)


def _causal_pairs(nq, newest_first):
    qi, kb = [], []
    for i in range(nq):
        ks = range(i, -1, -1) if newest_first else range(i + 1)
        qi += [i] * (i + 1)
        kb += list(ks)
    return jnp.asarray(qi, jnp.int32), jnp.asarray(kb, jnp.int32)


def _sb_prompt_kernel(qi_ref, kb_ref, qt_ref, kt_ref, vt_ref, tcat_ref, o_ref, q_sc, carry_sc, acc_sc, *, pairs):
    step_id = pl.program_id(1)
    i = qi_ref[step_id]
    kb = kb_ref[step_id]
    tq = qt_ref.shape[2]
    tk = kt_ref.shape[2]

    @pl.when(kb == i)
    def _():
        first = lax.broadcasted_iota(jnp.int32, (tq, LANES), 1) < LANES // 2
        for p in range(pairs):
            qp = qt_ref[0, p * LANES:(p + 1) * LANES, :].T
            q_sc[2 * p] = jnp.where(first, qp, 0.0).astype(BF16)
            q_sc[2 * p + 1] = jnp.where(first, 0.0, qp).astype(BF16)
        carry_sc[...] = jnp.zeros_like(carry_sc)
        acc_sc[...] = jnp.zeros_like(acc_sc)

    def step(masked):
        valid = None
        if masked:
            valid = (lax.broadcasted_iota(jnp.int32, (tq, tk), 1)
                     < lax.broadcasted_iota(jnp.int32, (tq, tk), 0))
        first = lax.broadcasted_iota(jnp.int32, (tq, LANES), 1) < LANES // 2
        tcat = tcat_ref[...]
        for p in range(pairs):
            kp = kt_ref[0, p * LANES:(p + 1) * LANES, :].astype(BF16)
            vp = vt_ref[0, p * LANES:(p + 1) * LANES, :].astype(BF16)
            pv = []
            for hh in range(2):
                hd = 2 * p + hh
                z = _dot(q_sc[hd], kp)
                w, carry = _sb_weights(z, carry_sc[hd], tcat, valid, newest_first=False)
                carry_sc[hd] = carry
                pv.append(_dot_nt(w.astype(BF16), vp))
            acc_sc[p] += jnp.where(first, pv[0], pv[1])

    pl.when(kb == i)(lambda: step(True))
    pl.when(kb < i)(lambda: step(False))

    @pl.when(kb == 0)
    def _():
        for p in range(pairs):
            o_ref[:, p * LANES:(p + 1) * LANES] = acc_sc[p].astype(o_ref.dtype)


def _sb_prompt(qt, kt, vt):
    nb, sbw, s = qt.shape
    assert sbw % LANES == 0
    pairs = sbw // LANES
    t = _largest_divisor(s, (256, 128))
    nq = s // t
    qi, kb = _causal_pairs(nq, newest_first=True)
    kern = functools.partial(_sb_prompt_kernel, pairs=pairs)
    kv_map = lambda b, p, qi, kb: (b, 0, kb[p])
    grid_spec = pltpu.PrefetchScalarGridSpec(
        num_scalar_prefetch=2, grid=(nb, qi.shape[0]),
        in_specs=[pl.BlockSpec((1, sbw, t), lambda b, p, qi, kb: (b, 0, qi[p])),
                  pl.BlockSpec((1, sbw, t), kv_map), pl.BlockSpec((1, sbw, t), kv_map),
                  _const_spec((2 * LANES, 2 * LANES))],
        out_specs=pl.BlockSpec((t, sbw), lambda b, p, qi, kb: (b * nq + qi[p], 0)),
        scratch_shapes=[pltpu.VMEM((2 * pairs, t, LANES), BF16), pltpu.VMEM((2 * pairs, t, LANES), F32),
                        pltpu.VMEM((pairs, t, LANES), F32)])
    return pl.pallas_call(
        kern, grid_spec=grid_spec, out_shape=jax.ShapeDtypeStruct((nb * s, sbw), BF16),
        compiler_params=_params("parallel", "arbitrary"), name="sb_prompt",
    )(qi, kb, qt, kt, vt, _sb_tcat())


def _softmax_update(s, v, m_sc, l_sc, acc_sc):
    reps_k = s.shape[1] // LANES
    m_prev = m_sc[...]
    m_new = jnp.maximum(m_prev, jnp.max(s, axis=1, keepdims=True))
    p = jnp.exp(s - _lane_tile(m_new, reps_k))
    alpha = jnp.exp(m_prev - m_new)
    l_sc[...] = alpha * l_sc[...] + jnp.sum(p, axis=1, keepdims=True)
    acc_sc[...] = acc_sc[...] * _lane_tile(alpha, acc_sc.shape[1] // LANES) + _dot(p.astype(BF16), v)
    m_sc[...] = m_new


def _mla_prompt_kernel(qi_ref, kb_ref, qlat_ref, qpe_ref, ckv_ref, kpe_ref, o_ref, m_sc, l_sc, acc_sc):
    step_id = pl.program_id(1)
    i = qi_ref[step_id]
    j = kb_ref[step_id]
    heads, tq, c = qlat_ref.shape
    tk = ckv_ref.shape[0]

    @pl.when(j == 0)
    def _():
        m_sc[...] = jnp.full_like(m_sc, NEG_BIG)
        l_sc[...] = jnp.zeros_like(l_sc)
        acc_sc[...] = jnp.zeros_like(acc_sc)

    def step(masked):
        ckv = ckv_ref[...].astype(BF16)
        s = (_dot_nt(qlat_ref[...].reshape(heads * tq, c), ckv)
             + _dot_nt(qpe_ref[...].reshape(heads * tq, LANES), kpe_ref[...]))
        if masked:
            s = s.reshape(heads, tq, tk)
            key = lax.broadcasted_iota(jnp.int32, (heads, tq, tk), 2)
            qry = lax.broadcasted_iota(jnp.int32, (heads, tq, tk), 1)
            s = jnp.where(key <= qry, s, -jnp.inf).reshape(heads * tq, tk)
        _softmax_update(s, ckv, m_sc, l_sc, acc_sc)

    pl.when(j < i)(lambda: step(False))
    pl.when(j == i)(lambda: step(True))

    @pl.when(j == i)
    def _():
        o = acc_sc[...] / _lane_tile(l_sc[...], c // LANES)
        o_ref[...] = o.reshape(heads, tq, c).astype(o_ref.dtype)


def _mla_prompt(qlat, qpe, ckv, kpe, nb):
    heads, n, c = qlat.shape
    assert c % LANES == 0
    s = n // nb
    t = _largest_divisor(s, (256, 128))
    nq = s // t
    qi, kb = _causal_pairs(nq, newest_first=False)
    q_map = lambda b, p, qi, kb: (0, b * nq + qi[p], 0)
    k_map = lambda b, p, qi, kb: (b * nq + kb[p], 0)
    grid_spec = pltpu.PrefetchScalarGridSpec(
        num_scalar_prefetch=2, grid=(nb, qi.shape[0]),
        in_specs=[pl.BlockSpec((heads, t, c), q_map), pl.BlockSpec((heads, t, LANES), q_map),
                  pl.BlockSpec((t, c), k_map), pl.BlockSpec((t, LANES), k_map)],
        out_specs=pl.BlockSpec((heads, t, c), q_map),
        scratch_shapes=[pltpu.VMEM((heads * t, LANES), F32), pltpu.VMEM((heads * t, LANES), F32),
                        pltpu.VMEM((heads * t, c), F32)])
    return pl.pallas_call(
        _mla_prompt_kernel, grid_spec=grid_spec, out_shape=jax.ShapeDtypeStruct((heads, n, c), BF16),
        compiler_params=_params("parallel", "arbitrary"), name="mla_prompt",
    )(qi, kb, qlat, qpe, ckv, kpe)


def _mem_kv_kernel(mem_ref, g_ref, wk_ref, wv_ref, k_ref, v_ref):
    mn = _rms(mem_ref[...], g_ref[...]).astype(BF16)
    k_ref[...] = _dot(mn, wk_ref[...])
    v_ref[...] = _dot(mn, wv_ref[...])


def _mem_kv(mem, g, wk, wv):
    n, d = mem.shape
    mw = wk.shape[1]
    tm = _largest_divisor(n, (512, 256, 128, 64, 32, 16, 8))
    row = lambda i: (i, 0)
    return pl.pallas_call(
        _mem_kv_kernel, grid=(n // tm,),
        in_specs=[pl.BlockSpec((tm, d), row), _const_spec((1, d)), _const_spec((d, mw)), _const_spec((d, mw))],
        out_specs=[pl.BlockSpec((tm, mw), row)] * 2,
        out_shape=[jax.ShapeDtypeStruct((n, mw), F32)] * 2,
        compiler_params=_params("parallel"), name="mem_kv",
    )(mem, g, wk, wv)


def _mem_attend_kernel(q_ref, k_ref, v_ref, o_ref, *, head_dim, interleaved):
    heads = q_ref.shape[1] // head_dim
    nb = k_ref.shape[0]
    tq = q_ref.shape[0] // nb
    for b in range(nb):
        for h in range(heads):
            c = h * head_dim
            if interleaved:
                rows = pl.ds(h, k_ref.shape[1] // heads, stride=heads)
                k, v = k_ref[b, rows, :], v_ref[b, rows, :]
            else:
                k, v = k_ref[b, :, c:c + head_dim], v_ref[b, :, c:c + head_dim]
            s = _dot_nt(q_ref[b * tq:(b + 1) * tq, c:c + head_dim].astype(BF16), k.astype(BF16))
            p = jnp.exp(s - jnp.max(s, axis=1, keepdims=True))
            o = _dot(p.astype(BF16), v.astype(BF16))
            o_ref[b * tq:(b + 1) * tq, c:c + head_dim] = o / jnp.sum(p, axis=1, keepdims=True)


def _mem_attend(mq, mk, mv, *, head_dim, interleaved):
    n, mw = mq.shape
    nb, m, kw = mk.shape
    assert head_dim == LANES and kw == (head_dim if interleaved else mw)
    t = n // nb
    tq = _largest_divisor(t, (512, 256, 128, 64, 32, 16, 8))
    nt = t // tq
    bpb = _largest_divisor(nb, (8, 4, 2, 1)) if nt == 1 and tq <= 64 else 1
    kern = functools.partial(_mem_attend_kernel, head_dim=head_dim, interleaved=interleaved)
    return pl.pallas_call(
        kern, grid=(nb // bpb, nt),
        in_specs=[pl.BlockSpec((bpb * tq, mw), lambda b, i: (b * nt + i, 0)),
                  pl.BlockSpec((bpb, m, kw), lambda b, i: (b, 0, 0)), pl.BlockSpec((bpb, m, kw), lambda b, i: (b, 0, 0))],
        out_specs=pl.BlockSpec((bpb * tq, mw), lambda b, i: (b * nt + i, 0)),
        out_shape=jax.ShapeDtypeStruct((n, mw), F32),
        compiler_params=_params("parallel", "parallel"), name="mem_attend",
    )(mq, mk, mv)


def _merge_kernel(x_ref, gate_ref, osb_ref, olat_ref, omem_ref, wuv_ref, wsb_ref, wmla_ref, wmem_ref, wo_ref, o_ref):
    d = x_ref.shape[1]
    heads = olat_ref.shape[0]
    parts = []
    for p in range(heads // 2):
        lhs = jnp.concatenate([olat_ref[2 * p].astype(BF16), olat_ref[2 * p + 1].astype(BF16)], axis=1)
        parts.append(_dot(lhs, wuv_ref[p]))
    o_mla = jnp.concatenate(parts, axis=1).astype(BF16)
    m = (gate_ref[:, 0:d].astype(F32) * _dot(osb_ref[...].astype(BF16), wsb_ref[...])
         + gate_ref[:, d:2 * d].astype(F32) * _dot(o_mla, wmla_ref[...])
         + gate_ref[:, 2 * d:3 * d].astype(F32) * _dot(omem_ref[...].astype(BF16), wmem_ref[...]))
    o_ref[...] = x_ref[...] + _dot(m.astype(BF16), wo_ref[...])


def _merge(x, gate, o_sb, o_lat, o_mem, w):
    n, d = x.shape
    heads, _, c = o_lat.shape
    tm = _largest_divisor(n, (512, 256, 128, 64, 32, 16, 8))
    row = lambda i: (i, 0)
    consts = [w["w_uv_pairs"], w["w_sb_out"], w["w_mla_out"], w["w_mem_out"], w["w_o"]]
    return pl.pallas_call(
        _merge_kernel, grid=(n // tm,),
        in_specs=[pl.BlockSpec((tm, d), row), pl.BlockSpec((tm, gate.shape[1]), row),
                  pl.BlockSpec((tm, o_sb.shape[1]), row), pl.BlockSpec((heads, tm, c), lambda i: (0, i, 0)),
                  pl.BlockSpec((tm, o_mem.shape[1]), row)] + [_const_spec(a.shape) for a in consts],
        out_specs=pl.BlockSpec((tm, d), row),
        out_shape=jax.ShapeDtypeStruct((n, d), F32),
        compiler_params=_params("parallel"), name="merge",
    )(x, gate, o_sb, o_lat, o_mem, *consts)


def _page_pipeline(pt_ref, caches, bufs, sem, *, group, n_pages, newest_first):
    b, s = pl.program_id(0), pl.program_id(1)
    steps = pl.num_programs(1)
    step = b * steps + s
    slot = lax.rem(step, 2)

    def copies(bb, ss, sl):
        out = []
        for g in range(group):
            j = ss * group + g
            page = pt_ref[bb * n_pages + (n_pages - 1 - j if newest_first else j)]
            for a, (cache, buf) in enumerate(zip(caches, bufs)):
                out.append(pltpu.make_async_copy(cache.at[page], buf.at[sl, g], sem.at[a, sl]))
        return out

    @pl.when(step == 0)
    def _():
        for cp in copies(b, s, slot):
            cp.start()

    @pl.when(step + 1 < pl.num_programs(0) * steps)
    def _():
        wrap = s + 1 == steps
        for cp in copies(jnp.where(wrap, b + 1, b), jnp.where(wrap, 0, s + 1), 1 - slot):
            cp.start()

    for cp in copies(b, s, slot):
        cp.wait()
    return slot


def _sb_sample_kernel(pt_ref, q_ref, knew_ref, vnew_ref, tcat_ref, ck_ref, cv_ref, o_ref,
                      kbuf, vbuf, sem, qbd_sc, carry_sc, acc_sc, *, group, heads, n_pages):
    s = pl.program_id(1)
    t, sbw = q_ref.shape
    rows = heads * t
    hd = sbw // heads
    tk = knew_ref.shape[2]
    tcat = tcat_ref[...]
    slot = _page_pipeline(pt_ref, (ck_ref, cv_ref), (kbuf, vbuf), sem, group=group, n_pages=n_pages,
                          newest_first=True)

    def own_head():
        row = lax.broadcasted_iota(jnp.int32, (rows, sbw), 0)
        col = lax.broadcasted_iota(jnp.int32, (rows, sbw), 1)
        mask = None
        for h in range(heads):
            m = (row >= h * t) & (row < (h + 1) * t) & (col >= h * hd) & (col < (h + 1) * hd)
            mask = m if mask is None else mask | m
        return mask

    def block(ks, vs, valid):
        kcat = jnp.concatenate([k.astype(BF16) for k in ks], axis=1)
        z = _dot(qbd_sc[...], kcat)
        w, carry = _sb_weights(z, carry_sc[...], tcat, valid, newest_first=True)
        carry_sc[...] = carry
        vcat = jnp.concatenate([v.astype(BF16) for v in vs], axis=1)
        acc_sc[...] += _dot_nt(w.astype(BF16), vcat)

    @pl.when(s == 0)
    def _():
        q = jnp.concatenate([q_ref[...]] * heads, axis=0)
        qbd_sc[...] = jnp.where(own_head(), q, 0.0).astype(BF16)
        carry_sc[...] = jnp.zeros_like(carry_sc)
        acc_sc[...] = jnp.zeros_like(acc_sc)
        key = lax.broadcasted_iota(jnp.int32, (rows, tk), 1)
        qry = lax.rem(lax.broadcasted_iota(jnp.int32, (rows, tk), 0), t)
        block([knew_ref[0]], [vnew_ref[0]], key < qry)

    block([kbuf[slot, g] for g in range(group)], [vbuf[slot, g] for g in range(group)], None)

    @pl.when(s == pl.num_programs(1) - 1)
    def _():
        diag = jnp.where(own_head(), acc_sc[...], 0.0).reshape(heads, t, sbw)
        o_ref[...] = jnp.sum(diag, axis=0)


def _sb_sample(q, knew, vnew, cache_k, cache_v, page_table, *, heads):
    n, sbw = q.shape
    nb, n_pages = page_table.shape
    t = n // nb
    page = cache_k.shape[2]
    assert page == LANES
    group = _largest_divisor(n_pages, (16, 8, 4, 2, 1))
    own = lambda b, s, pt: (b, 0)
    new = lambda b, s, pt: (b, 0, 0)
    kern = functools.partial(_sb_sample_kernel, group=group, heads=heads, n_pages=n_pages)
    grid_spec = pltpu.PrefetchScalarGridSpec(
        num_scalar_prefetch=1, grid=(nb, n_pages // group),
        in_specs=[pl.BlockSpec((t, sbw), own), pl.BlockSpec((1, sbw, page), new), pl.BlockSpec((1, sbw, page), new),
                  pl.BlockSpec((2 * LANES, 2 * LANES), lambda b, s, pt: (0, 0)),
                  pl.BlockSpec(memory_space=pl.ANY), pl.BlockSpec(memory_space=pl.ANY)],
        out_specs=pl.BlockSpec((t, sbw), own),
        scratch_shapes=[pltpu.VMEM((2, group, sbw, page), F32), pltpu.VMEM((2, group, sbw, page), F32),
                        pltpu.SemaphoreType.DMA((2, 2)),
                        pltpu.VMEM((heads * t, sbw), BF16), pltpu.VMEM((heads * t, LANES), F32),
                        pltpu.VMEM((heads * t, sbw), F32)])
    return pl.pallas_call(
        kern, grid_spec=grid_spec, out_shape=jax.ShapeDtypeStruct((n, sbw), F32),
        compiler_params=_params("arbitrary", "arbitrary"), name="sb_sample",
    )(page_table.reshape(-1), q, knew, vnew, _sb_tcat(), cache_k, cache_v)


def _mla_sample_kernel(pt_ref, qlat_ref, qpe_ref, cnew_ref, pnew_ref, cc_ref, cp_ref, o_ref,
                       cbuf, pbuf, sem, m_sc, l_sc, acc_sc, *, group, n_pages):
    s = pl.program_id(1)
    heads, t, c = qlat_ref.shape
    rows = heads * t
    rope, tk = pnew_ref.shape[1:]
    qlat = qlat_ref[...].reshape(rows, c).astype(BF16)
    qpe = qpe_ref[...].reshape(rows, LANES).astype(BF16)
    slot = _page_pipeline(pt_ref, (cc_ref, cp_ref), (cbuf, pbuf), sem, group=group, n_pages=n_pages,
                          newest_first=False)

    def scores(ckv, kpets):
        kpe = jnp.concatenate([p.astype(BF16) for p in kpets], axis=1)
        kpe = jnp.concatenate([kpe, jnp.zeros((LANES - rope, kpe.shape[1]), BF16)], axis=0)
        return _dot_nt(qlat, ckv) + _dot(qpe, kpe)

    @pl.when(s == 0)
    def _():
        m_sc[...] = jnp.full_like(m_sc, NEG_BIG)
        l_sc[...] = jnp.zeros_like(l_sc)
        acc_sc[...] = jnp.zeros_like(acc_sc)
        ckv = cnew_ref[0].astype(BF16)
        key = lax.broadcasted_iota(jnp.int32, (rows, tk), 1)
        qry = lax.rem(lax.broadcasted_iota(jnp.int32, (rows, tk), 0), t)
        sc = jnp.where(key <= qry, scores(ckv, [pnew_ref[0]]), -jnp.inf)
        _softmax_update(sc, ckv, m_sc, l_sc, acc_sc)

    ckv = cbuf[slot].reshape(group * tk, c).astype(BF16)
    _softmax_update(scores(ckv, [pbuf[slot, g] for g in range(group)]), ckv, m_sc, l_sc, acc_sc)

    @pl.when(s == pl.num_programs(1) - 1)
    def _():
        o = acc_sc[...] / _lane_tile(l_sc[...], c // LANES)
        o_ref[...] = o.reshape(heads, t, c)


def _mla_sample(qlat, qpe, cnew, pnew, cache_ckv, cache_kpet, page_table):
    heads, n, c = qlat.shape
    nb, n_pages = page_table.shape
    t = n // nb
    page = cache_ckv.shape[1]
    rope = cache_kpet.shape[1]
    assert page == LANES
    group = _largest_divisor(n_pages, (32, 16, 8, 4, 2, 1))
    own = lambda b, s, pt: (0, b, 0)
    new = lambda b, s, pt: (b, 0, 0)
    kern = functools.partial(_mla_sample_kernel, group=group, n_pages=n_pages)
    grid_spec = pltpu.PrefetchScalarGridSpec(
        num_scalar_prefetch=1, grid=(nb, n_pages // group),
        in_specs=[pl.BlockSpec((heads, t, c), own), pl.BlockSpec((heads, t, LANES), own),
                  pl.BlockSpec((1, page, c), new), pl.BlockSpec((1, rope, page), new),
                  pl.BlockSpec(memory_space=pl.ANY), pl.BlockSpec(memory_space=pl.ANY)],
        out_specs=pl.BlockSpec((heads, t, c), own),
        scratch_shapes=[pltpu.VMEM((2, group, page, c), F32), pltpu.VMEM((2, group, rope, page), F32),
                        pltpu.SemaphoreType.DMA((2, 2)),
                        pltpu.VMEM((heads * t, LANES), F32), pltpu.VMEM((heads * t, LANES), F32),
                        pltpu.VMEM((heads * t, c), F32)])
    return pl.pallas_call(
        kern, grid_spec=grid_spec, out_shape=jax.ShapeDtypeStruct((heads, n, c), F32),
        compiler_params=_params("arbitrary", "arbitrary"), name="mla_sample",
    )(page_table.reshape(-1), qlat, qpe, cnew, pnew, cache_ckv, cache_kpet)


def _swap_halves(w):
    half = w.shape[-1] // 2
    return jnp.concatenate([w[..., half:], w[..., :half]], axis=-1)


def _pad_lanes(w):
    return jnp.pad(w, [(0, 0)] * (w.ndim - 1) + [(0, LANES - w.shape[-1])])


def _layer_weights(l, dims, w_in, b_gate, q_norm, kv_norm, w_uq, w_uk, w_uv, norm_mix, w_sb_out, w_mla_out, w_mem_out,
                   w_o):
    sbw, q_rank, kv_rank, rope, mem_w = (dims[k] for k in ("sbw", "q_rank", "kv_rank", "rope", "mem_w"))
    heads, nope, vdim = dims["mla_heads"], dims["mla_nope"], dims["mla_v"]
    w = w_in[l]
    edges = [0]
    for size in (sbw, sbw, sbw, q_rank, kv_rank, rope, mem_w):
        edges.append(edges[-1] + size)
    wq, wk, wv, w_cq, w_ckv, w_kpe, w_mq = (w[:, a:b] for a, b in zip(edges[:-1], edges[1:]))
    w_g = w[:, edges[-1]:]
    w_kpe_sw = _swap_halves(w_kpe)
    uq = w_uq[l].reshape(q_rank, heads, nope + rope)
    uq_pe = uq[:, :, nope:]
    uv = w_uv[l]
    uv_pairs = jnp.zeros((heads // 2, 2, kv_rank, 2, vdim), uv.dtype)
    for hh in range(2):
        uv_pairs = uv_pairs.at[:, hh, :, hh, :].set(uv[:, hh::2, :].transpose(1, 0, 2))
    return {
        "norm_mix": norm_mix[l][None, :],
        "w_t": jnp.concatenate([wq, wk, wv, w_kpe, w_kpe_sw], axis=1).T.astype(BF16),
        "w_rest": jnp.concatenate([w_cq, w_ckv, _pad_lanes(w_kpe), _pad_lanes(w_kpe_sw), w_mq], axis=1).astype(BF16),
        "w_g": w_g.astype(BF16),
        "b_g": b_gate[l][None, :],
        "q_norm": q_norm[l][None, :],
        "kv_norm": kv_norm[l][None, :],
        "w_uq_nope": uq[:, :, :nope].reshape(q_rank, heads * nope).astype(BF16),
        "w_uq_pe": _pad_lanes(uq_pe).reshape(q_rank, heads * LANES).astype(BF16),
        "w_uq_pe_sw": _pad_lanes(_swap_halves(uq_pe)).reshape(q_rank, heads * LANES).astype(BF16),
        "w_uk_pairs": w_uk[l].transpose(1, 2, 0).reshape(heads // 2, 2 * nope, kv_rank).astype(BF16),
        "w_uv_pairs": uv_pairs.reshape(heads // 2, 2 * kv_rank, 2 * vdim).astype(BF16),
        "w_sb_out": w_sb_out[l].astype(BF16),
        "w_mla_out": w_mla_out[l].astype(BF16),
        "w_mem_out": w_mem_out[l].astype(BF16),
        "w_o": w_o[l].astype(BF16),
    }


def _pad_page(x, page):
    return jnp.pad(x, [(0, 0)] * (x.ndim - 1) + [(0, page - x.shape[-1])])


def kernel(x_prompt, x_sample, cache_sb_k, cache_sb_v, cache_mla_ckv, cache_mla_kpe, cache_mem_k, cache_mem_v, page_table, mem_prompt, w_in, b_gate, q_norm, kv_norm, w_uq, w_uk, w_uv, mem_norm, w_mk, w_mv, w_sb_out, w_mla_out, w_mem_out, w_o, norm_ffn1, w1_gate, w1_up, w1_down, norm_mix, norm_ffn2, w2_gate, w2_up, w2_down, norm_final):
    nb, seq, d = x_prompt.shape
    db, dt, _ = x_sample.shape
    depth, n_pool, page, sb_heads, sb_d = cache_sb_k.shape
    n_pages = page_table.shape[1]
    mem_tokens, mem_heads, mem_d = cache_mem_k.shape[2:]
    dims = {
        "sbw": sb_heads * sb_d, "sb_d": sb_d, "q_rank": q_norm.shape[1], "kv_rank": cache_mla_ckv.shape[-1],
        "rope": cache_mla_kpe.shape[-1], "mem_w": mem_heads * mem_d, "mem_d": mem_d,
        "mla_heads": w_uk.shape[2], "mla_nope": w_uk.shape[3], "mla_v": w_uv.shape[3],
    }
    sbw, kv_rank, rope, mem_w, heads = (dims[k] for k in ("sbw", "kv_rank", "rope", "mem_w", "mla_heads"))
    pos_p = jnp.arange(seq)
    pos_s = n_pages * page + jnp.arange(dt)
    gf = norm_final[None, :]

    xp = x_prompt.reshape(nb * seq, d)
    xs = x_sample.reshape(db * dt, d)
    outs = [[] for _ in range(10)]
    for l in range(depth):
        ffn1 = (norm_ffn1[l][None, :], w1_gate[l].astype(BF16), w1_up[l].astype(BF16), w1_down[l].astype(BF16), gf)
        ffn2 = (norm_ffn2[l][None, :], w2_gate[l].astype(BF16), w2_up[l].astype(BF16), w2_down[l].astype(BF16), gf)
        w = _layer_weights(l, dims, w_in, b_gate, q_norm, kv_norm, w_uq, w_uk, w_uv, norm_mix, w_sb_out, w_mla_out,
                           w_mem_out, w_o)
        last = l == depth - 1
        xp = _ffn(xp, *ffn1, final=False)
        xs = _ffn(xs, *ffn1, final=False)

        qt, kt, vt, kpet, ckv, kpe, qlat, qpe, mq, gate = _project(xp, nb, pos_p, w, act_dtype=BF16, dims=dims)
        o_sb = _sb_prompt(qt, kt, vt)
        o_lat = _mla_prompt(qlat, qpe, ckv, kpe, nb)
        mk, mv = _mem_kv(mem_prompt.reshape(nb * mem_tokens, d), mem_norm[l][None, :],
                         w_mk[l].reshape(d, mem_w).astype(BF16), w_mv[l].reshape(d, mem_w).astype(BF16))
        o_mem = _mem_attend(mq, mk.reshape(nb, mem_tokens, mem_w), mv.reshape(nb, mem_tokens, mem_w), head_dim=mem_d,
                            interleaved=False)
        xp = _merge(xp, gate, o_sb, o_lat, o_mem, w)
        outs[0].append(kt.reshape(nb, sb_heads, sb_d, seq).transpose(0, 3, 1, 2))
        outs[1].append(vt.reshape(nb, sb_heads, sb_d, seq).transpose(0, 3, 1, 2))
        outs[2].append(ckv.reshape(nb, seq, kv_rank))
        outs[3].append(kpet.transpose(0, 2, 1))
        outs[4].append(mk.reshape(nb, mem_tokens, mem_heads, mem_d))
        outs[5].append(mv.reshape(nb, mem_tokens, mem_heads, mem_d))

        qt, kt, vt, kpet, ckv, kpe, qlat, qpe, mq, gate = _project(xs, 1, jnp.tile(pos_s, db), w, act_dtype=F32, dims=dims)
        per_batch = lambda a: a.reshape(a.shape[1], db, dt).transpose(1, 0, 2)
        knew, vnew, pnew = per_batch(kt), per_batch(vt), per_batch(kpet)
        cache_kt = cache_sb_k[l].transpose(0, 2, 3, 1).reshape(n_pool, sbw, page)
        cache_vt = cache_sb_v[l].transpose(0, 2, 3, 1).reshape(n_pool, sbw, page)
        o_sb = _sb_sample(qt[0].T, _pad_page(knew, page), _pad_page(vnew, page), cache_kt, cache_vt, page_table,
                          heads=sb_heads)
        cnew = jnp.pad(ckv.reshape(db, dt, kv_rank), ((0, 0), (0, page - dt), (0, 0)))
        o_lat = _mla_sample(qlat, qpe, cnew, _pad_page(pnew, page), cache_mla_ckv[l],
                            cache_mla_kpe[l].transpose(0, 2, 1), page_table)
        o_mem = _mem_attend(mq, cache_mem_k[l].reshape(db, mem_tokens * mem_heads, mem_d),
                            cache_mem_v[l].reshape(db, mem_tokens * mem_heads, mem_d), head_dim=mem_d, interleaved=True)
        xs = _merge(xs, gate, o_sb, o_lat, o_mem, w)
        outs[6].append(knew.reshape(db, sb_heads, sb_d, dt).transpose(0, 3, 1, 2))
        outs[7].append(vnew.reshape(db, sb_heads, sb_d, dt).transpose(0, 3, 1, 2))
        outs[8].append(ckv.reshape(db, dt, kv_rank))
        outs[9].append(pnew.transpose(0, 2, 1))

        xp = _ffn(xp, *ffn2, final=last)
        xs = _ffn(xs, *ffn2, final=last)

    return (xp.reshape(nb, seq, d), xs.reshape(db, dt, d)) + tuple(jnp.stack(o) for o in outs)
```
